```python
import math
import jax
import jax.numpy as jnp
from jax import lax
import numpy as np

D_MODEL = 4096
BATCH = 1
SEQ = 8192
DEPTH = 2

MIX_WIDTH = D_MODEL
ATTN_WIDTH = MIX_WIDTH // 2
SSM_WIDTH = MIX_WIDTH - ATTN_WIDTH

HEAD_DIM = 128
N_HEADS = ATTN_WIDTH // HEAD_DIM
N_KV = 4
HEADS_PER_KV = N_HEADS // N_KV
KV_WIDTH = N_KV * HEAD_DIM
ROPE_THETA = 10000.0
CMP_LEN = 32
CMP_STRIDE = 16
SEL_BLOCK = 64
SEL_TOP = 16
WINDOW = 512
Q_BLOCK = 128
BIG = 1e9

SSM_HEAD_DIM = 64
SSM_HEADS = SSM_WIDTH // SSM_HEAD_DIM
SSM_GROUPS = 8
SSM_STATE = 128
CONV_WIDTH = 4
SSD_CHUNK = 256
CONV_DIM = SSM_WIDTH + 2 * SSM_GROUPS * SSM_STATE

D_FF = 4 * D_MODEL
EPS = 1e-5
NEG = -1e30

IN_SPLITS = (ATTN_WIDTH, KV_WIDTH, KV_WIDTH, KV_WIDTH, KV_WIDTH, KV_WIDTH, KV_WIDTH,
             3 * N_HEADS, SSM_WIDTH, CONV_DIM, SSM_HEADS)
IN_COLS = sum(IN_SPLITS)
IN_OFFSETS = tuple(int(v) for v in np.cumsum(IN_SPLITS)[:-1])

kernel_name = "hymba_nsa_ssd_hybrid_trunk"


def rms_norm(x, w):
    xf = x.astype(jnp.float32)
    y = xf * lax.rsqrt(jnp.mean(xf * xf, axis=-1, keepdims=True) + EPS)
    return (y * w.astype(jnp.float32)).astype(x.dtype)


def rope_tables(seq):
    inv = 1.0 / (ROPE_THETA ** (jnp.arange(0, HEAD_DIM, 2, dtype=jnp.float32) / HEAD_DIM))
    ang = jnp.arange(seq, dtype=jnp.float32)[:, None] * inv[None, :]
    ang = jnp.concatenate([ang, ang], axis=-1)
    return jnp.cos(ang), jnp.sin(ang)


def apply_rope(t, cos, sin):
    tf = t.astype(jnp.float32)
    half = HEAD_DIM // 2
    rot = jnp.concatenate([-tf[..., half:], tf[..., :half]], axis=-1)
    return (tf * cos[:, None, :] + rot * sin[:, None, :]).astype(t.dtype)


def masked_softmax(s, mask):
    s = jnp.where(mask, s, NEG)
    m = jnp.max(s, axis=-1, keepdims=True)
    p = jnp.exp(s - m) * mask
    return p / jnp.maximum(jnp.sum(p, axis=-1, keepdims=True), 1e-30)


def compress_blocks(t, pos, w1, w2):
    b, g, s, d = t.shape
    seg = t.reshape(b, g, s // CMP_STRIDE, CMP_STRIDE, d)
    blocks = jnp.concatenate([seg[:, :, :-1], seg[:, :, 1:]], axis=3)
    blocks = blocks + pos
    flat = blocks.reshape(b, g, blocks.shape[2], CMP_LEN * d)
    return jax.nn.silu(flat @ w1) @ w2


def nsa_mixer(q, k_c, v_c, k_s, v_s, k_w, v_w, gate_logits, cmp_pos, cmp_w1, cmp_w2, cos, sin):
    b, s, _ = q.shape
    q = apply_rope(q.reshape(b, s, N_HEADS, HEAD_DIM), cos, sin)
    q = q.reshape(b, s, N_KV, HEADS_PER_KV, HEAD_DIM).transpose(0, 2, 3, 1, 4)

    def heads(t, rope):
        t = t.reshape(b, s, N_KV, HEAD_DIM)
        if rope:
            t = apply_rope(t, cos, sin)
        return t.transpose(0, 2, 1, 3)

    kc = compress_blocks(heads(k_c, True), cmp_pos[0], cmp_w1[0], cmp_w2[0])
    vc = compress_blocks(heads(v_c, False), cmp_pos[1], cmp_w1[1], cmp_w2[1])
    n_cmp = kc.shape[2]
    n_sel = s // SEL_BLOCK
    top = min(SEL_TOP, n_sel)
    ks_blk = heads(k_s, True).reshape(b, N_KV, n_sel, SEL_BLOCK, HEAD_DIM)
    vs_blk = heads(v_s, False).reshape(b, N_KV, n_sel, SEL_BLOCK, HEAD_DIM)
    pad = ((0, 0), (0, 0), (WINDOW, 0), (0, 0))
    kw_pad = jnp.pad(heads(k_w, True), pad)
    vw_pad = jnp.pad(heads(v_w, False), pad)
    gates = jax.nn.sigmoid(gate_logits.astype(jnp.float32))
    gates = gates.reshape(b, s, 3, N_KV, HEADS_PER_KV).transpose(2, 0, 3, 4, 1)

    c_start = jnp.arange(n_cmp) * CMP_STRIDE
    c_end = c_start + CMP_LEN - 1
    j_start = jnp.arange(n_sel) * SEL_BLOCK
    overlap = ((c_start[:, None] < j_start[None, :] + SEL_BLOCK)
               & (c_start[:, None] + CMP_LEN > j_start[None, :])).astype(jnp.float32)
    bi = jnp.arange(b)[:, None, None, None]
    gi = jnp.arange(N_KV)[None, :, None, None]
    blk = jnp.arange(n_sel)[None, :]
    scale = HEAD_DIM ** -0.5

    def one_block(i):
        t0 = i * Q_BLOCK
        tpos = t0 + jnp.arange(Q_BLOCK)
        qb = lax.dynamic_slice_in_dim(q, t0, Q_BLOCK, axis=3)
        gb = lax.dynamic_slice_in_dim(gates, t0, Q_BLOCK, axis=4)
        sc = jnp.einsum('bgrqd,bgcd->bgrqc', qb, kc).astype(jnp.float32) * scale
        pc = masked_softmax(sc, c_end[None, :] <= tpos[:, None])
        o_cmp = jnp.einsum('bgrqc,bgcd->bgrqd', pc.astype(vc.dtype), vc)
        imp = jnp.einsum('bgrqc,cj->bgqj', pc, overlap)
        cur = (tpos // SEL_BLOCK)[:, None]
        forced = (blk == 0) | (blk == cur) | (blk == cur - 1)
        imp = jnp.where(forced, BIG, imp)
        imp = jnp.where(blk > cur, -BIG, imp)
        _, idx = lax.top_k(imp, top)
        ksel = ks_blk[bi, gi, idx].reshape(b, N_KV, Q_BLOCK, top * SEL_BLOCK, HEAD_DIM)
        vsel = vs_blk[bi, gi, idx].reshape(b, N_KV, Q_BLOCK, top * SEL_BLOCK, HEAD_DIM)
        kpos = (idx[..., None] * SEL_BLOCK + jnp.arange(SEL_BLOCK)).reshape(b, N_KV, Q_BLOCK, top * SEL_BLOCK)
        ss = jnp.einsum('bgrqd,bgqkd->bgrqk', qb, ksel).astype(jnp.float32) * scale
        ps = masked_softmax(ss, (kpos <= tpos[None, None, :, None])[:, :, None])
        o_sel = jnp.einsum('bgrqk,bgqkd->bgrqd', ps.astype(vsel.dtype), vsel)
        kwb = lax.dynamic_slice_in_dim(kw_pad, t0, Q_BLOCK + WINDOW, axis=2)
        vwb = lax.dynamic_slice_in_dim(vw_pad, t0, Q_BLOCK + WINDOW, axis=2)
        wpos = t0 - WINDOW + jnp.arange(Q_BLOCK + WINDOW)
        wmask = ((wpos[None, :] <= tpos[:, None]) & (wpos[None, :] > tpos[:, None] - WINDOW)
                 & (wpos[None, :] >= 0))
        sw = jnp.einsum('bgrqd,bgkd->bgrqk', qb, kwb).astype(jnp.float32) * scale
        pw = masked_softmax(sw, wmask)
        o_win = jnp.einsum('bgrqk,bgkd->bgrqd', pw.astype(vwb.dtype), vwb)
        o = gb[0][..., None] * o_cmp + gb[1][..., None] * o_sel + gb[2][..., None] * o_win
        return o.astype(q.dtype)

    out = lax.map(one_block, jnp.arange(s // Q_BLOCK))
    return out.transpose(1, 0, 4, 2, 3, 5).reshape(b, s, ATTN_WIDTH)


def ssd_scan(x, a, bm, cm):
    b, s, h, p = x.shape
    n = bm.shape[-1]
    chunk = math.gcd(s, SSD_CHUNK)
    nc = s // chunk
    r = h // SSM_GROUPS
    x = x.reshape(b, nc, chunk, SSM_GROUPS, r, p)
    a = a.reshape(b, nc, chunk, SSM_GROUPS, r).transpose(0, 3, 4, 1, 2)
    bm = bm.reshape(b, nc, chunk, SSM_GROUPS, n)
    cm = cm.reshape(b, nc, chunk, SSM_GROUPS, n)
    a_cs = jnp.cumsum(a, axis=-1)
    tril = jnp.tril(jnp.ones((chunk, chunk), dtype=bool))
    lmat = jnp.exp(jnp.where(tril, a_cs[..., :, None] - a_cs[..., None, :], -jnp.inf))
    cb = jnp.einsum('bclgn,bcsgn->bcgls', cm, bm)
    y_diag = jnp.einsum('bcgls,bgrcls,bcsgrp->bclgrp', cb, lmat, x)
    decay_states = jnp.exp(a_cs[..., -1:] - a_cs)
    states = jnp.einsum('bcsgn,bgrcs,bcsgrp->cbgrpn', bm, decay_states, x)
    chunk_decay = jnp.exp(a_cs[..., -1]).transpose(3, 0, 1, 2)

    def step(hc, inp):
        st, dec = inp
        return dec[..., None, None] * hc + st, hc

    h0 = jnp.zeros(states.shape[1:], jnp.float32)
    _, prev = lax.scan(step, h0, (states, chunk_decay))
    y_off = jnp.einsum('bclgn,cbgrpn,bgrcl->bclgrp', cm, prev, jnp.exp(a_cs))
    return (y_diag + y_off).reshape(b, s, h, p)


def ssd_mixer(z, xbc, dt, conv_w, conv_b, dt_bias, a_log, d_skip, norm_w):
    b, s, _ = xbc.shape
    xpad = jnp.pad(xbc, ((0, 0), (CONV_WIDTH - 1, 0), (0, 0)))
    conv = conv_b
    for k in range(CONV_WIDTH):
        conv = conv + xpad[:, k:k + s] * conv_w[k]
    xbc = jax.nn.silu(conv)
    xs, bm, cm = jnp.split(xbc, [SSM_WIDTH, SSM_WIDTH + SSM_GROUPS * SSM_STATE], axis=-1)
    xs = xs.reshape(b, s, SSM_HEADS, SSM_HEAD_DIM)
    bm = bm.reshape(b, s, SSM_GROUPS, SSM_STATE)
    cm = cm.reshape(b, s, SSM_GROUPS, SSM_STATE)
    dt = jax.nn.softplus(dt.astype(jnp.float32) + dt_bias.astype(jnp.float32))
    a = -jnp.exp(a_log.astype(jnp.float32))
    y = ssd_scan(xs * dt[..., None], dt * a, bm, cm)
    y = y + d_skip.astype(jnp.float32)[:, None] * xs
    y = y.reshape(b, s, SSM_WIDTH) * jax.nn.silu(z.astype(jnp.float32))
    yg = y.reshape(b, s, SSM_GROUPS, SSM_WIDTH // SSM_GROUPS)
    yg = yg * lax.rsqrt(jnp.mean(yg * yg, axis=-1, keepdims=True) + EPS)
    return (yg.reshape(b, s, SSM_WIDTH) * norm_w.astype(jnp.float32)).astype(z.dtype)


def setup_inputs(seed: int = 0) -> dict:
    key = jax.random.key(seed)
    ks = jax.random.split(key, 20)

    def normal(k, shape, scale):
        return jax.random.normal(k, shape, jnp.float32) * scale

    dt0 = jnp.exp(jax.random.uniform(ks[9], (DEPTH, SSM_HEADS), jnp.float32)
                  * (math.log(0.1) - math.log(0.001)) + math.log(0.001))
    return {
        'x': normal(ks[0], (BATCH, SEQ, D_MODEL), 1.0),
        'norm_mix': 1.0 + normal(ks[1], (DEPTH, D_MODEL), 0.02),
        'w_in': normal(ks[2], (DEPTH, D_MODEL, IN_COLS), D_MODEL ** -0.5),
        'cmp_pos': normal(ks[3], (DEPTH, 2, CMP_LEN, HEAD_DIM), 0.1),
        'cmp_w1': normal(ks[4], (DEPTH, 2, CMP_LEN * HEAD_DIM, HEAD_DIM), (CMP_LEN * HEAD_DIM) ** -0.5),
        'cmp_w2': normal(ks[5], (DEPTH, 2, HEAD_DIM, HEAD_DIM), HEAD_DIM ** -0.5),
        'attn_norm': 1.0 + normal(ks[6], (DEPTH, ATTN_WIDTH), 0.02),
        'conv_w': normal(ks[7], (DEPTH, CONV_WIDTH, CONV_DIM), CONV_WIDTH ** -0.5),
        'conv_b': normal(ks[8], (DEPTH, CONV_DIM), 0.01),
        'dt_bias': dt0 + jnp.log(-jnp.expm1(-dt0)),
        'a_log': jnp.log(jax.random.uniform(ks[10], (DEPTH, SSM_HEADS), jnp.float32, 1.0, 16.0)),
        'd_skip': 1.0 + normal(ks[11], (DEPTH, SSM_HEADS), 0.02),
        'ssm_norm': 1.0 + normal(ks[12], (DEPTH, SSM_WIDTH), 0.02),
        'w_out': normal(ks[13], (DEPTH, MIX_WIDTH, D_MODEL), MIX_WIDTH ** -0.5),
        'norm_mlp': 1.0 + normal(ks[14], (DEPTH, D_MODEL), 0.02),
        'w_up': normal(ks[15], (DEPTH, D_MODEL, D_FF), D_MODEL ** -0.5),
        'w_down': normal(ks[16], (DEPTH, D_FF, D_MODEL), D_FF ** -0.5),
        'norm_final': 1.0 + normal(ks[17], (D_MODEL,), 0.02),
    }


def reference(x, norm_mix, w_in, cmp_pos, cmp_w1, cmp_w2, attn_norm, conv_w, conv_b, dt_bias,
              a_log, d_skip, ssm_norm, w_out, norm_mlp, w_up, w_down, norm_final):
    seq = x.shape[1]
    cos, sin = rope_tables(seq)
    h = x
    for l in range(DEPTH):
        xn = rms_norm(h, norm_mix[l])
        proj = xn @ w_in[l]
        q, k_c, v_c, k_s, v_s, k_w, v_w, g_logits, z, xbc, dt = jnp.split(proj, IN_OFFSETS, axis=-1)
        attn = nsa_mixer(q, k_c, v_c, k_s, v_s, k_w, v_w, g_logits,
                         cmp_pos[l], cmp_w1[l], cmp_w2[l], cos, sin)
        attn = rms_norm(attn, attn_norm[l])
        ssm = ssd_mixer(z, xbc, dt, conv_w[l], conv_b[l], dt_bias[l], a_log[l], d_skip[l], ssm_norm[l])
        h = h + jnp.concatenate([attn, ssm], axis=-1) @ w_out[l]
        hn = rms_norm(h, norm_mlp[l])
        h = h + jnp.square(jax.nn.relu(hn @ w_up[l])) @ w_down[l]
    return rms_norm(h, norm_final)
```

```python
import functools
import math

import jax
import jax.numpy as jnp
import numpy as np
from jax import lax
from jax.experimental import pallas as pl
from jax.experimental.pallas import tpu as pltpu

F32 = jnp.float32
BF16 = jnp.bfloat16

D_MODEL = 4096
DEPTH = 2
ATTN_WIDTH = 2048
SSM_WIDTH = 2048
HEAD_DIM = 128
N_HEADS = 16
N_KV = 4
HEADS_PER_KV = 4
KV_WIDTH = 512
ROPE_THETA = 10000.0
CMP_LEN = 32
CMP_STRIDE = 16
SEL_BLOCK = 64
SEL_TOP = 16
WINDOW = 512
Q_BLOCK = 128
BIG = 1e9
SSM_HEAD_DIM = 64
SSM_HEADS = 32
SSM_GROUPS = 8
SSM_STATE = 128
CONV_WIDTH = 4
SSD_CHUNK = 256
CONV_DIM = SSM_WIDTH + 2 * SSM_GROUPS * SSM_STATE
D_FF = 4 * D_MODEL
EPS = 1e-5
NEG = -1e30

VMEM_LIMIT_BYTES = 52 * 1024 * 1024

ATTN_COLS = ATTN_WIDTH + 6 * KV_WIDTH
GATE_COLS = 3 * N_HEADS
Z_OFF = ATTN_COLS + GATE_COLS
DT_OFF = Z_OFF + SSM_WIDTH + CONV_DIM
SMALL_COLS = 128


def _matmul_body(*refs, nk, act, has_res):
    a_ref, b_ref = refs[0], refs[1]
    r_ref = refs[2] if has_res else None
    o_ref = refs[3] if has_res else refs[2]
    part = jnp.dot(a_ref[...], b_ref[...], preferred_element_type=F32)

    def finish(acc):
        if act == "relu2":
            acc = jnp.square(jnp.maximum(acc, 0.0))
        if has_res:
            acc = acc + r_ref[...]
        o_ref[...] = acc.astype(o_ref.dtype)

    if nk == 1:
        finish(part)
    else:
        acc_ref = refs[-1]
        k = pl.program_id(2)

        @pl.when(k == 0)
        def _():
            acc_ref[...] = part

        @pl.when(k > 0)
        def _():
            acc_ref[...] += part

        @pl.when(k == nk - 1)
        def _():
            finish(acc_ref[...])


def matmul(a, b, *, residual=None, act=None, out_dtype=F32, bm=1024, bn=1024, bk=None, name):
    m, kdim = a.shape
    _, n = b.shape
    bk = kdim if bk is None else bk
    bn = min(bn, n)
    assert m % bm == 0 and n % bn == 0 and kdim % bk == 0
    nk = kdim // bk
    in_specs = [pl.BlockSpec((bm, bk), lambda i, j, k: (i, k)),
                pl.BlockSpec((bk, bn), lambda i, j, k: (k, j))]
    args = [a, b]
    if residual is not None:
        in_specs.append(pl.BlockSpec((bm, bn), lambda i, j, k: (i, j)))
        args.append(residual)
    return pl.pallas_call(
        functools.partial(_matmul_body, nk=nk, act=act, has_res=residual is not None),
        grid=(m // bm, n // bn, nk),
        in_specs=in_specs,
        out_specs=pl.BlockSpec((bm, bn), lambda i, j, k: (i, j)),
        out_shape=jax.ShapeDtypeStruct((m, n), out_dtype),
        scratch_shapes=[pltpu.VMEM((bm, bn), F32)] if nk > 1 else [],
        compiler_params=pltpu.CompilerParams(
            dimension_semantics=("parallel", "parallel", "arbitrary"),
            vmem_limit_bytes=VMEM_LIMIT_BYTES),
        name=name,
    )(*args)


def _rmsnorm_body(x_ref, w_ref, o_ref):
    x = x_ref[...]
    y = x * lax.rsqrt(jnp.mean(x * x, axis=-1, keepdims=True) + EPS)
    o_ref[...] = (y * w_ref[...]).astype(o_ref.dtype)


def rmsnorm(x, w, *, out_dtype, rows=256, name):
    m, d = x.shape
    return pl.pallas_call(
        _rmsnorm_body,
        grid=(m // rows,),
        in_specs=[pl.BlockSpec((rows, d), lambda i: (i, 0)),
                  pl.BlockSpec((1, d), lambda i: (0, 0))],
        out_specs=pl.BlockSpec((rows, d), lambda i: (i, 0)),
        out_shape=jax.ShapeDtypeStruct((m, d), out_dtype),
        compiler_params=pltpu.CompilerParams(dimension_semantics=("parallel",)),
        name=name,
    )(x, w.reshape(1, d))


def _rms_norm_jax(x, w):
    xf = x.astype(F32)
    y = xf * lax.rsqrt(jnp.mean(xf * xf, axis=-1, keepdims=True) + EPS)
    return (y * w.astype(F32)).astype(x.dtype)


def _rope_tables(seq):
    inv = 1.0 / (ROPE_THETA ** (jnp.arange(0, HEAD_DIM, 2, dtype=F32) / HEAD_DIM))
    ang = jnp.arange(seq, dtype=F32)[:, None] * inv[None, :]
    ang = jnp.concatenate([ang, ang], axis=-1)
    return jnp.cos(ang), jnp.sin(ang)


def _apply_rope(t, cos, sin):
    tf = t.astype(F32)
    half = HEAD_DIM // 2
    rot = jnp.concatenate([-tf[..., half:], tf[..., :half]], axis=-1)
    return (tf * cos[:, None, :] + rot * sin[:, None, :]).astype(t.dtype)


def _masked_softmax(s, mask):
    s = jnp.where(mask, s, NEG)
    m = jnp.max(s, axis=-1, keepdims=True)
    p = jnp.exp(s - m) * mask
    return p / jnp.maximum(jnp.sum(p, axis=-1, keepdims=True), 1e-30)


def _compress_blocks(t, pos, w1, w2):
    b, g, s, d = t.shape
    seg = t.reshape(b, g, s // CMP_STRIDE, CMP_STRIDE, d)
    blocks = jnp.concatenate([seg[:, :, :-1], seg[:, :, 1:]], axis=3)
    blocks = blocks + pos
    flat = blocks.reshape(b, g, blocks.shape[2], CMP_LEN * d)
    return jax.nn.silu(flat @ w1) @ w2


def _nsa_mixer(q, k_c, v_c, k_s, v_s, k_w, v_w, gate_logits, cmp_pos, cmp_w1, cmp_w2, cos, sin):
    b, s, _ = q.shape
    q = _apply_rope(q.reshape(b, s, N_HEADS, HEAD_DIM), cos, sin)
    q = q.reshape(b, s, N_KV, HEADS_PER_KV, HEAD_DIM).transpose(0, 2, 3, 1, 4)

    def heads(t, rope):
        t = t.reshape(b, s, N_KV, HEAD_DIM)
        if rope:
            t = _apply_rope(t, cos, sin)
        return t.transpose(0, 2, 1, 3)

    kc = _compress_blocks(heads(k_c, True), cmp_pos[0], cmp_w1[0], cmp_w2[0])
    vc = _compress_blocks(heads(v_c, False), cmp_pos[1], cmp_w1[1], cmp_w2[1])
    n_cmp = kc.shape[2]
    n_sel = s // SEL_BLOCK
    top = min(SEL_TOP, n_sel)
    ks_blk = heads(k_s, True).reshape(b, N_KV, n_sel, SEL_BLOCK, HEAD_DIM)
    vs_blk = heads(v_s, False).reshape(b, N_KV, n_sel, SEL_BLOCK, HEAD_DIM)
    pad = ((0, 0), (0, 0), (WINDOW, 0), (0, 0))
    kw_pad = jnp.pad(heads(k_w, True), pad)
    vw_pad = jnp.pad(heads(v_w, False), pad)
    gates = jax.nn.sigmoid(gate_logits.astype(F32))
    gates = gates.reshape(b, s, 3, N_KV, HEADS_PER_KV).transpose(2, 0, 3, 4, 1)

    c_start = jnp.arange(n_cmp) * CMP_STRIDE
    c_end = c_start + CMP_LEN - 1
    j_start = jnp.arange(n_sel) * SEL_BLOCK
    overlap = ((c_start[:, None] < j_start[None, :] + SEL_BLOCK)
               & (c_start[:, None] + CMP_LEN > j_start[None, :])).astype(F32)
    bi = jnp.arange(b)[:, None, None, None]
    gi = jnp.arange(N_KV)[None, :, None, None]
    blk = jnp.arange(n_sel)[None, :]
    scale = HEAD_DIM ** -0.5

    def one_block(i):
        t0 = i * Q_BLOCK
        tpos = t0 + jnp.arange(Q_BLOCK)
        qb = lax.dynamic_slice_in_dim(q, t0, Q_BLOCK, axis=3)
        gb = lax.dynamic_slice_in_dim(gates, t0, Q_BLOCK, axis=4)
        sc = jnp.einsum('bgrqd,bgcd->bgrqc', qb, kc).astype(F32) * scale
        pc = _masked_softmax(sc, c_end[None, :] <= tpos[:, None])
        o_cmp = jnp.einsum('bgrqc,bgcd->bgrqd', pc.astype(vc.dtype), vc)
        imp = jnp.einsum('bgrqc,cj->bgqj', pc, overlap)
        cur = (tpos // SEL_BLOCK)[:, None]
        forced = (blk == 0) | (blk == cur) | (blk == cur - 1)
        imp = jnp.where(forced, BIG, imp)
        imp = jnp.where(blk > cur, -BIG, imp)
        _, idx = lax.top_k(imp, top)
        ksel = ks_blk[bi, gi, idx].reshape(b, N_KV, Q_BLOCK, top * SEL_BLOCK, HEAD_DIM)
        vsel = vs_blk[bi, gi, idx].reshape(b, N_KV, Q_BLOCK, top * SEL_BLOCK, HEAD_DIM)
        kpos = (idx[..., None] * SEL_BLOCK + jnp.arange(SEL_BLOCK)).reshape(b, N_KV, Q_BLOCK, top * SEL_BLOCK)
        ss = jnp.einsum('bgrqd,bgqkd->bgrqk', qb, ksel).astype(F32) * scale
        ps = _masked_softmax(ss, (kpos <= tpos[None, None, :, None])[:, :, None])
        o_sel = jnp.einsum('bgrqk,bgqkd->bgrqd', ps.astype(vsel.dtype), vsel)
        kwb = lax.dynamic_slice_in_dim(kw_pad, t0, Q_BLOCK + WINDOW, axis=2)
        vwb = lax.dynamic_slice_in_dim(vw_pad, t0, Q_BLOCK + WINDOW, axis=2)
        wpos = t0 - WINDOW + jnp.arange(Q_BLOCK + WINDOW)
        wmask = ((wpos[None, :] <= tpos[:, None]) & (wpos[None, :] > tpos[:, None] - WINDOW)
                 & (wpos[None, :] >= 0))
        sw = jnp.einsum('bgrqd,bgkd->bgrqk', qb, kwb).astype(F32) * scale
        pw = _masked_softmax(sw, wmask)
        o_win = jnp.einsum('bgrqk,bgkd->bgrqd', pw.astype(vwb.dtype), vwb)
        o = gb[0][..., None] * o_cmp + gb[1][..., None] * o_sel + gb[2][..., None] * o_win
        return o.astype(q.dtype)

    out = lax.map(one_block, jnp.arange(s // Q_BLOCK))
    return out.transpose(1, 0, 4, 2, 3, 5).reshape(b, s, ATTN_WIDTH)


def _ssd_scan(x, a, bm, cm):
    b, s, h, p = x.shape
    n = bm.shape[-1]
    chunk = math.gcd(s, SSD_CHUNK)
    nc = s // chunk
    r = h // SSM_GROUPS
    x = x.reshape(b, nc, chunk, SSM_GROUPS, r, p)
    a = a.reshape(b, nc, chunk, SSM_GROUPS, r).transpose(0, 3, 4, 1, 2)
    bm = bm.reshape(b, nc, chunk, SSM_GROUPS, n)
    cm = cm.reshape(b, nc, chunk, SSM_GROUPS, n)
    a_cs = jnp.cumsum(a, axis=-1)
    tril = jnp.tril(jnp.ones((chunk, chunk), dtype=bool))
    lmat = jnp.exp(jnp.where(tril, a_cs[..., :, None] - a_cs[..., None, :], -jnp.inf))
    cb = jnp.einsum('bclgn,bcsgn->bcgls', cm, bm)
    y_diag = jnp.einsum('bcgls,bgrcls,bcsgrp->bclgrp', cb, lmat, x)
    decay_states = jnp.exp(a_cs[..., -1:] - a_cs)
    states = jnp.einsum('bcsgn,bgrcs,bcsgrp->cbgrpn', bm, decay_states, x)
    chunk_decay = jnp.exp(a_cs[..., -1]).transpose(3, 0, 1, 2)

    def step(hc, inp):
        st, dec = inp
        return dec[..., None, None] * hc + st, hc

    h0 = jnp.zeros(states.shape[1:], F32)
    _, prev = lax.scan(step, h0, (states, chunk_decay))
    y_off = jnp.einsum('bclgn,cbgrpn,bgrcl->bclgrp', cm, prev, jnp.exp(a_cs))
    return (y_diag + y_off).reshape(b, s, h, p)


def _ssd_mixer(z, xbc, dt, conv_w, conv_b, dt_bias, a_log, d_skip, norm_w):
    b, s, _ = xbc.shape
    xpad = jnp.pad(xbc, ((0, 0), (CONV_WIDTH - 1, 0), (0, 0)))
    conv = conv_b
    for k in range(CONV_WIDTH):
        conv = conv + xpad[:, k:k + s] * conv_w[k]
    xbc = jax.nn.silu(conv)
    xs, bm, cm = jnp.split(xbc, [SSM_WIDTH, SSM_WIDTH + SSM_GROUPS * SSM_STATE], axis=-1)
    xs = xs.reshape(b, s, SSM_HEADS, SSM_HEAD_DIM)
    bm = bm.reshape(b, s, SSM_GROUPS, SSM_STATE)
    cm = cm.reshape(b, s, SSM_GROUPS, SSM_STATE)
    dt = jax.nn.softplus(dt.astype(F32) + dt_bias.astype(F32))
    a = -jnp.exp(a_log.astype(F32))
    y = _ssd_scan(xs * dt[..., None], dt * a, bm, cm)
    y = y + d_skip.astype(F32)[:, None] * xs
    y = y.reshape(b, s, SSM_WIDTH) * jax.nn.silu(z.astype(F32))
    yg = y.reshape(b, s, SSM_GROUPS, SSM_WIDTH // SSM_GROUPS)
    yg = yg * lax.rsqrt(jnp.mean(yg * yg, axis=-1, keepdims=True) + EPS)
    return (yg.reshape(b, s, SSM_WIDTH) * norm_w.astype(F32)).astype(z.dtype)


def kernel(x, norm_mix, w_in, cmp_pos, cmp_w1, cmp_w2, attn_norm, conv_w, conv_b, dt_bias, a_log,
           d_skip, ssm_norm, w_out, norm_mlp, w_up, w_down, norm_final):
    batch, seq, _ = x.shape
    assert batch == 1
    cos, sin = _rope_tables(seq)
    h = x.reshape(seq, D_MODEL)
    for l in range(DEPTH):
        w_big = jnp.concatenate([w_in[l, :, :ATTN_COLS], w_in[l, :, Z_OFF:DT_OFF]], axis=1).astype(BF16)
        w_small = jnp.concatenate(
            [w_in[l, :, ATTN_COLS:Z_OFF], w_in[l, :, DT_OFF:],
             jnp.zeros((D_MODEL, SMALL_COLS - GATE_COLS - SSM_HEADS), F32)], axis=1).astype(BF16)
        xn = rmsnorm(h, norm_mix[l], out_dtype=BF16, name=f"norm_mix{l}")
        proj = matmul(xn, w_big, name=f"in_proj{l}")
        small = matmul(xn, w_small, name=f"in_proj_small{l}")
        offs = np.cumsum([0, ATTN_WIDTH] + [KV_WIDTH] * 6 + [SSM_WIDTH, CONV_DIM])
        q, k_c, v_c, k_s, v_s, k_w, v_w, z, xbc = [proj[None, :, offs[i]:offs[i + 1]] for i in range(9)]
        g_logits = small[None, :, :GATE_COLS]
        dt = small[None, :, GATE_COLS:GATE_COLS + SSM_HEADS]
        attn = _nsa_mixer(q, k_c, v_c, k_s, v_s, k_w, v_w, g_logits, cmp_pos[l], cmp_w1[l], cmp_w2[l], cos, sin)
        attn = _rms_norm_jax(attn, attn_norm[l])
        ssm = _ssd_mixer(z, xbc, dt, conv_w[l], conv_b[l], dt_bias[l], a_log[l], d_skip[l], ssm_norm[l])
        mix = jnp.concatenate([attn, ssm], axis=-1).reshape(seq, D_MODEL).astype(BF16)
        h = matmul(mix, w_out[l].astype(BF16), residual=h, bn=512, name=f"out_proj{l}")
        hn = rmsnorm(h, norm_mlp[l], out_dtype=BF16, name=f"norm_mlp{l}")
        u = matmul(hn, w_up[l].astype(BF16), act="relu2", out_dtype=BF16, name=f"mlp_up{l}")
        h = matmul(u, w_down[l].astype(BF16), residual=h, bk=2048, name=f"mlp_down{l}")
    out = rmsnorm(h, norm_final, out_dtype=F32, name="norm_final")
    return out.reshape(batch, seq, D_MODEL)
```

```python
import functools

import jax
import jax.numpy as jnp
import numpy as np
from jax import lax
from jax.experimental import pallas as pl
from jax.experimental.pallas import tpu as pltpu

F32 = jnp.float32
BF16 = jnp.bfloat16

D_MODEL = 4096
DEPTH = 2
ATTN_WIDTH = 2048
SSM_WIDTH = 2048
HEAD_DIM = 128
N_HEADS = 16
N_KV = 4
HEADS_PER_KV = 4
KV_WIDTH = 512
ROPE_THETA = 10000.0
CMP_LEN = 32
CMP_STRIDE = 16
SEL_BLOCK = 64
SEL_TOP = 16
WINDOW = 512
Q_BLOCK = 128
BIG = 1e9
SSM_HEAD_DIM = 64
SSM_HEADS = 32
SSM_GROUPS = 8
SSM_STATE = 128
CONV_WIDTH = 4
SSD_CHUNK = 256
CONV_DIM = SSM_WIDTH + 2 * SSM_GROUPS * SSM_STATE
D_FF = 4 * D_MODEL
EPS = 1e-5
NEG = -1e30

VMEM_LIMIT_BYTES = 52 * 1024 * 1024

GATE_COLS = 3 * N_HEADS
W_KV_OFF = ATTN_WIDTH
W_GATE_OFF = W_KV_OFF + 6 * KV_WIDTH
W_Z_OFF = W_GATE_OFF + GATE_COLS
W_XBC_OFF = W_Z_OFF + SSM_WIDTH
W_DT_OFF = W_XBC_OFF + CONV_DIM
Z_COL = ATTN_WIDTH
XBC_COL = Z_COL + SSM_WIDTH
KC_OFF = XBC_COL + CONV_DIM
VC_OFF, KS_OFF, VS_OFF, KW_OFF, VW_OFF = (KC_OFF + i * KV_WIDTH for i in range(1, 6))
SMALL_COLS = 128


def _matmul_body(*refs, nk, act, has_res):
    a_ref, b_ref = refs[0], refs[1]
    r_ref = refs[2] if has_res else None
    o_ref = refs[3] if has_res else refs[2]
    part = jnp.dot(a_ref[...], b_ref[...], preferred_element_type=F32)

    def finish(acc):
        if act == "relu2":
            acc = jnp.square(jnp.maximum(acc, 0.0))
        if has_res:
            acc = acc + r_ref[...]
        o_ref[...] = acc.astype(o_ref.dtype)

    if nk == 1:
        finish(part)
    else:
        acc_ref = refs[-1]
        k = pl.program_id(2)

        @pl.when(k == 0)
        def _():
            acc_ref[...] = part

        @pl.when(k > 0)
        def _():
            acc_ref[...] += part

        @pl.when(k == nk - 1)
        def _():
            finish(acc_ref[...])


def matmul(a, b, *, residual=None, act=None, out_dtype=F32, bm=1024, bn=1024, bk=None, name):
    m, kdim = a.shape
    _, n = b.shape
    bk = kdim if bk is None else bk
    bn = min(bn, n)
    assert m % bm == 0 and n % bn == 0 and kdim % bk == 0
    nk = kdim // bk
    in_specs = [pl.BlockSpec((bm, bk), lambda i, j, k: (i, k)),
                pl.BlockSpec((bk, bn), lambda i, j, k: (k, j))]
    args = [a, b]
    if residual is not None:
        in_specs.append(pl.BlockSpec((bm, bn), lambda i, j, k: (i, j)))
        args.append(residual)
    return pl.pallas_call(
        functools.partial(_matmul_body, nk=nk, act=act, has_res=residual is not None),
        grid=(m // bm, n // bn, nk),
        in_specs=in_specs,
        out_specs=pl.BlockSpec((bm, bn), lambda i, j, k: (i, j)),
        out_shape=jax.ShapeDtypeStruct((m, n), out_dtype),
        scratch_shapes=[pltpu.VMEM((bm, bn), F32)] if nk > 1 else [],
        compiler_params=pltpu.CompilerParams(
            dimension_semantics=("parallel", "parallel", "arbitrary"),
            vmem_limit_bytes=VMEM_LIMIT_BYTES),
        name=name,
    )(*args)


def _rmsnorm_body(x_ref, w_ref, o_ref):
    x = x_ref[...]
    y = x * lax.rsqrt(jnp.mean(x * x, axis=-1, keepdims=True) + EPS)
    o_ref[...] = (y * w_ref[...]).astype(o_ref.dtype)


def rmsnorm(x, w, *, out_dtype, rows=256, name):
    m, d = x.shape
    return pl.pallas_call(
        _rmsnorm_body,
        grid=(m // rows,),
        in_specs=[pl.BlockSpec((rows, d), lambda i: (i, 0)),
                  pl.BlockSpec((1, d), lambda i: (0, 0))],
        out_specs=pl.BlockSpec((rows, d), lambda i: (i, 0)),
        out_shape=jax.ShapeDtypeStruct((m, d), out_dtype),
        compiler_params=pltpu.CompilerParams(dimension_semantics=("parallel",)),
        name=name,
    )(x, w.reshape(1, d))


def _norm_concat_body(a_ref, w_ref, s_ref, o_ref):
    x = a_ref[...]
    y = x * lax.rsqrt(jnp.mean(x * x, axis=-1, keepdims=True) + EPS)
    o_ref[:, :ATTN_WIDTH] = (y * w_ref[...]).astype(o_ref.dtype)
    o_ref[:, ATTN_WIDTH:] = s_ref[...]


def norm_concat(attn, w, ssm, *, rows=256, name):
    m = attn.shape[0]
    return pl.pallas_call(
        _norm_concat_body,
        grid=(m // rows,),
        in_specs=[pl.BlockSpec((rows, ATTN_WIDTH), lambda i: (i, 0)),
                  pl.BlockSpec((1, ATTN_WIDTH), lambda i: (0, 0)),
                  pl.BlockSpec((rows, SSM_WIDTH), lambda i: (i, 0))],
        out_specs=pl.BlockSpec((rows, D_MODEL), lambda i: (i, 0)),
        out_shape=jax.ShapeDtypeStruct((m, D_MODEL), BF16),
        compiler_params=pltpu.CompilerParams(dimension_semantics=("parallel",)),
        name=name,
    )(attn, w.reshape(1, ATTN_WIDTH), ssm)


SEL_CHUNK = 256


def _split3(x):
    hi = x.astype(BF16)
    r1 = x - hi.astype(F32)
    mid = r1.astype(BF16)
    lo = (r1 - mid.astype(F32)).astype(BF16)
    return hi, mid, lo


def _dot_nt(a, b):
    return lax.dot_general(a, b, (((1,), (1,)), ((), ())), preferred_element_type=F32)


def _prep_body(q_ref, ks_ref, vs_ref, kw_ref, vw_ref, cos_ref, sin_ref, qo, kso, vso, kwo, vwo):
    cos = cos_ref[...]
    sin = sin_ref[...]
    scale = HEAD_DIM ** -0.5

    def rope(t):
        return t * cos + pltpu.roll(t, HEAD_DIM // 2, 1) * sin

    for h in range(N_HEADS):
        sl = slice(h * HEAD_DIM, (h + 1) * HEAD_DIM)
        qo[:, sl] = (rope(q_ref[:, sl]) * scale).astype(BF16)
    ones = jnp.ones((q_ref.shape[0], HEAD_DIM), BF16)
    for g in range(N_KV):
        sl = slice(g * HEAD_DIM, (g + 1) * HEAD_DIM)
        kso[:, sl] = rope(ks_ref[:, sl]).astype(BF16)
        kwo[:, sl] = rope(kw_ref[:, sl]).astype(BF16)
        vso[:, 2 * g * HEAD_DIM:(2 * g + 1) * HEAD_DIM] = vs_ref[:, sl].astype(BF16)
        vso[:, (2 * g + 1) * HEAD_DIM:(2 * g + 2) * HEAD_DIM] = ones
        vwo[:, 2 * g * HEAD_DIM:(2 * g + 1) * HEAD_DIM] = vw_ref[:, sl].astype(BF16)
        vwo[:, (2 * g + 1) * HEAD_DIM:(2 * g + 2) * HEAD_DIM] = ones


def nsa_prep(proj, cos, sin_signed, *, rows=512, name):
    seq = proj.shape[0]
    kvb = lambda off: pl.BlockSpec((rows, KV_WIDTH), lambda i, o=off // KV_WIDTH: (i, o))
    return pl.pallas_call(
        _prep_body,
        grid=(seq // rows,),
        in_specs=[pl.BlockSpec((rows, ATTN_WIDTH), lambda i: (i, 0)),
                  kvb(KS_OFF), kvb(VS_OFF), kvb(KW_OFF), kvb(VW_OFF),
                  pl.BlockSpec((rows, HEAD_DIM), lambda i: (i, 0)),
                  pl.BlockSpec((rows, HEAD_DIM), lambda i: (i, 0))],
        out_specs=[pl.BlockSpec((rows, ATTN_WIDTH), lambda i: (i, 0)),
                   pl.BlockSpec((rows, KV_WIDTH), lambda i: (i, 0)),
                   pl.BlockSpec((rows, 2 * KV_WIDTH), lambda i: (i, 0)),
                   pl.BlockSpec((rows, KV_WIDTH), lambda i: (i, 0)),
                   pl.BlockSpec((rows, 2 * KV_WIDTH), lambda i: (i, 0))],
        out_shape=[jax.ShapeDtypeStruct((seq, ATTN_WIDTH), BF16),
                   jax.ShapeDtypeStruct((seq, KV_WIDTH), BF16),
                   jax.ShapeDtypeStruct((seq, 2 * KV_WIDTH), BF16),
                   jax.ShapeDtypeStruct((seq, KV_WIDTH), BF16),
                   jax.ShapeDtypeStruct((seq, 2 * KV_WIDTH), BF16)],
        compiler_params=pltpu.CompilerParams(dimension_semantics=("parallel",),
                                             vmem_limit_bytes=VMEM_LIMIT_BYTES),
        name=name,
    )(proj, proj, proj, proj, proj, cos, sin_signed)


def _compress_body(t_ref, cos_ref, sin_ref, pos_ref, w1_ref, w2_ref, o_ref, tr_ref, *, rope, aug):
    seq = t_ref.shape[0]
    ncp = seq // CMP_STRIDE
    t = t_ref[...]
    if rope:
        t = t * cos_ref[...] + pltpu.roll(t, HEAD_DIM // 2, 1) * sin_ref[...]
    tr_ref[...] = t
    acc_a = jnp.zeros((ncp, HEAD_DIM), F32)
    acc_b = jnp.zeros((ncp, HEAD_DIM), F32)
    for p in range(CMP_STRIDE):
        tp = tr_ref[pl.ds(p, ncp, stride=CMP_STRIDE), :]
        wa = w1_ref[p * HEAD_DIM:(p + 1) * HEAD_DIM, :].astype(BF16)
        wb = w1_ref[(CMP_STRIDE + p) * HEAD_DIM:(CMP_STRIDE + p + 1) * HEAD_DIM, :].astype(BF16)
        acc_a += jnp.dot((tp + pos_ref[p:p + 1, :]).astype(BF16), wa, preferred_element_type=F32)
        acc_b += jnp.dot((tp + pos_ref[CMP_STRIDE + p:CMP_STRIDE + p + 1, :]).astype(BF16), wb,
                         preferred_element_type=F32)
    pre = acc_a + pltpu.roll(acc_b, ncp - 1, 0)
    hid = pre * jax.nn.sigmoid(pre)
    out = jnp.dot(hid.astype(BF16), w2_ref[...].astype(BF16), preferred_element_type=F32)
    o_ref[0, :, :HEAD_DIM] = out.astype(BF16)
    if aug:
        o_ref[0, :, HEAD_DIM:] = jnp.ones((ncp, HEAD_DIM), BF16)


def nsa_compress(proj, col_off, cos, sin_signed, pos, w1, w2, *, rope, aug, name):
    seq = proj.shape[0]
    ncp = seq // CMP_STRIDE
    width = 2 * HEAD_DIM if aug else HEAD_DIM
    return pl.pallas_call(
        functools.partial(_compress_body, rope=rope, aug=aug),
        grid=(N_KV,),
        in_specs=[pl.BlockSpec((seq, HEAD_DIM), lambda g, o=col_off // HEAD_DIM: (0, o + g)),
                  pl.BlockSpec((seq, HEAD_DIM), lambda g: (0, 0)),
                  pl.BlockSpec((seq, HEAD_DIM), lambda g: (0, 0)),
                  pl.BlockSpec((CMP_LEN, HEAD_DIM), lambda g: (0, 0)),
                  pl.BlockSpec((CMP_LEN * HEAD_DIM, HEAD_DIM), lambda g: (0, 0)),
                  pl.BlockSpec((HEAD_DIM, HEAD_DIM), lambda g: (0, 0))],
        out_specs=pl.BlockSpec((1, ncp, width), lambda g: (g, 0, 0)),
        out_shape=jax.ShapeDtypeStruct((N_KV, ncp, width), BF16),
        scratch_shapes=[pltpu.VMEM((seq, HEAD_DIM), F32)],
        compiler_params=pltpu.CompilerParams(dimension_semantics=("parallel",),
                                             vmem_limit_bytes=VMEM_LIMIT_BYTES),
        name=name,
    )(proj, cos, sin_signed, pos, w1, w2)


def _nsa_body(q_ref, gate_ref, kc_ref, vc_ref, ks_ref, vs_ref, kw_ref, vw_ref, ovt_ref, o_ref,
              sc_ref, m_ref, acc_ref, *, seq):
    i = pl.program_id(1)
    t0 = i * Q_BLOCK
    ncp = seq // CMP_STRIDE
    nsel = seq // SEL_BLOCK
    top = min(SEL_TOP, nsel)
    nwin = WINDOW + Q_BLOCK
    rows = [slice(r * Q_BLOCK, (r + 1) * Q_BLOCK) for r in range(HEADS_PER_KV)]

    q4 = jnp.concatenate([q_ref[:, r * HEAD_DIM:(r + 1) * HEAD_DIM] for r in range(HEADS_PER_KV)], axis=0)

    s_c = _dot_nt(q4, kc_ref[0])
    c_end = lax.broadcasted_iota(jnp.int32, (Q_BLOCK, ncp), 1) * CMP_STRIDE + (CMP_LEN - 1)
    tq_c = t0 + lax.broadcasted_iota(jnp.int32, (Q_BLOCK, ncp), 0)
    cmask = c_end <= tq_c
    vc = vc_ref[0]
    psum = jnp.zeros((Q_BLOCK, ncp), F32)
    o_cmp = []
    for r in range(HEADS_PER_KV):
        s = jnp.where(cmask, s_c[rows[r]], NEG)
        m = jnp.max(s, axis=-1, keepdims=True)
        p = jnp.where(cmask, jnp.exp(s - m), 0.0)
        pn = p / jnp.maximum(jnp.sum(p, axis=-1, keepdims=True), 1e-30)
        psum = psum + pn
        o_cmp.append(jnp.dot(pn.astype(BF16), vc[:, :HEAD_DIM], preferred_element_type=F32))

    ovt = ovt_ref[...]
    imp = sum(_dot_nt(ovt, part) for part in _split3(psum))
    blk = lax.broadcasted_iota(jnp.int32, (nsel, Q_BLOCK), 0)
    cur = (t0 + lax.broadcasted_iota(jnp.int32, (nsel, Q_BLOCK), 1)) // SEL_BLOCK
    blkf = blk.astype(F32)
    imp = jnp.where(blk == 0, BIG, jnp.where(blk == cur, BIG, jnp.where(blk == cur - 1, BIG, imp)))
    imp = jnp.where(blk > cur, -BIG, imp)
    sel_t = jnp.zeros((nsel, Q_BLOCK), F32)
    for _ in range(top):
        mx = jnp.max(imp, axis=0, keepdims=True)
        first = jnp.min(jnp.where(imp == mx, blkf, float(nsel)), axis=0, keepdims=True)
        hit = blkf == first
        sel_t = jnp.where(hit, 1.0, sel_t)
        imp = jnp.where(hit, -jnp.inf, imp)
    sel_t = jnp.where(blk <= cur, sel_t, 0.0)
    sel = sel_t.T.astype(BF16)

    n_chunks = (i + 2) // 2
    diff = (lax.broadcasted_iota(jnp.int32, (nsel, SEL_CHUNK), 0)
            - lax.broadcasted_iota(jnp.int32, (nsel, SEL_CHUNK), 1) // SEL_BLOCK)
    lane_k = lax.broadcasted_iota(jnp.int32, (Q_BLOCK, SEL_CHUNK), 1)
    tq_s = t0 + lax.broadcasted_iota(jnp.int32, (Q_BLOCK, SEL_CHUNK), 0)
    acc_ref[...] = jnp.full(acc_ref.shape, NEG, F32)

    def pass1(c, carry):
        k0 = pl.multiple_of(c * SEL_CHUNK, SEL_CHUNK)
        s = _dot_nt(q4, ks_ref[pl.ds(k0, SEL_CHUNK), :])
        expand = jnp.where(diff == c * (SEL_CHUNK // SEL_BLOCK), 1.0, 0.0).astype(BF16)
        picked = jnp.dot(sel, expand, preferred_element_type=F32)
        picked = jnp.where(k0 + lane_k <= tq_s, picked, 0.0)
        mask = picked > 0.5
        for r in range(HEADS_PER_KV):
            sm = jnp.where(mask, s[rows[r]], NEG)
            sc_ref[c, rows[r], :] = sm
            acc_ref[rows[r], :] = jnp.maximum(acc_ref[rows[r], :], sm)
        return carry

    lax.fori_loop(0, n_chunks, pass1, 0)
    m_ref[...] = jnp.broadcast_to(jnp.max(acc_ref[...], axis=-1, keepdims=True), m_ref.shape)

    acc_ref[...] = jnp.zeros(acc_ref.shape, F32)

    def pass2(c, carry):
        k0 = pl.multiple_of(c * SEL_CHUNK, SEL_CHUNK)
        p = jnp.exp(sc_ref[c] - m_ref[...]).astype(BF16)
        acc_ref[...] += jnp.dot(p, vs_ref[pl.ds(k0, SEL_CHUNK), :], preferred_element_type=F32)
        return carry

    lax.fori_loop(0, n_chunks, pass2, 0)

    w0 = pl.multiple_of(t0, Q_BLOCK)
    s_w = _dot_nt(q4, kw_ref[pl.ds(w0, nwin), :])
    vwin = vw_ref[pl.ds(w0, nwin), :]
    wpos = t0 - WINDOW + lax.broadcasted_iota(jnp.int32, (Q_BLOCK, nwin), 1)
    tq_w = t0 + lax.broadcasted_iota(jnp.int32, (Q_BLOCK, nwin), 0)
    wvalid = jnp.where(wpos <= tq_w, jnp.where(wpos > tq_w - WINDOW, jnp.where(wpos >= 0, 1.0, 0.0), 0.0), 0.0)
    wmask = wvalid > 0.5

    gate = jax.nn.sigmoid(gate_ref[0])
    for r in range(HEADS_PER_KV):
        sm = jnp.where(wmask, s_w[rows[r]], NEG)
        pw = jnp.exp(sm - jnp.max(sm, axis=-1, keepdims=True)).astype(BF16)
        res_w = jnp.dot(pw, vwin, preferred_element_type=F32)
        o_win = res_w[:, :HEAD_DIM] / res_w[:, HEAD_DIM:]
        res_s = acc_ref[rows[r], :]
        o_sel = res_s[:, :HEAD_DIM] / res_s[:, HEAD_DIM:]
        g_c = gate[:, r:r + 1]
        g_s = gate[:, HEADS_PER_KV + r:HEADS_PER_KV + r + 1]
        g_w = gate[:, 2 * HEADS_PER_KV + r:2 * HEADS_PER_KV + r + 1]
        o_ref[:, r * HEAD_DIM:(r + 1) * HEAD_DIM] = g_c * o_cmp[r] + g_s * o_sel + g_w * o_win


def _overlap_t(seq):
    ncp = seq // CMP_STRIDE
    nsel = seq // SEL_BLOCK
    c_start = np.arange(ncp) * CMP_STRIDE
    j_start = np.arange(nsel) * SEL_BLOCK
    ov = ((c_start[None, :] < j_start[:, None] + SEL_BLOCK) & (c_start[None, :] + CMP_LEN > j_start[:, None])
          & (np.arange(ncp)[None, :] < ncp - 1))
    return jnp.asarray(ov.astype(np.float32), BF16)


def nsa_attention(q_r, gates_g, kc, vc_aug, ks_r, vs_aug, kw_pad, vw_pad, *, name):
    seq = q_r.shape[0]
    nq = seq // Q_BLOCK
    ncp = seq // CMP_STRIDE
    grp = HEADS_PER_KV * HEAD_DIM
    return pl.pallas_call(
        functools.partial(_nsa_body, seq=seq),
        grid=(N_KV, nq),
        in_specs=[pl.BlockSpec((Q_BLOCK, grp), lambda g, i: (i, g)),
                  pl.BlockSpec((1, Q_BLOCK, HEAD_DIM), lambda g, i: (g, i, 0)),
                  pl.BlockSpec((1, ncp, HEAD_DIM), lambda g, i: (g, 0, 0)),
                  pl.BlockSpec((1, ncp, 2 * HEAD_DIM), lambda g, i: (g, 0, 0)),
                  pl.BlockSpec((seq, HEAD_DIM), lambda g, i: (0, g)),
                  pl.BlockSpec((seq, 2 * HEAD_DIM), lambda g, i: (0, g)),
                  pl.BlockSpec((seq + WINDOW, HEAD_DIM), lambda g, i: (0, g)),
                  pl.BlockSpec((seq + WINDOW, 2 * HEAD_DIM), lambda g, i: (0, g)),
                  pl.BlockSpec((seq // SEL_BLOCK, ncp), lambda g, i: (0, 0))],
        out_specs=pl.BlockSpec((Q_BLOCK, grp), lambda g, i: (i, g)),
        out_shape=jax.ShapeDtypeStruct((seq, ATTN_WIDTH), F32),
        scratch_shapes=[pltpu.VMEM((seq // SEL_CHUNK, HEADS_PER_KV * Q_BLOCK, SEL_CHUNK), F32),
                        pltpu.VMEM((HEADS_PER_KV * Q_BLOCK, SEL_CHUNK), F32),
                        pltpu.VMEM((HEADS_PER_KV * Q_BLOCK, SEL_CHUNK), F32)],
        compiler_params=pltpu.CompilerParams(dimension_semantics=("parallel", "arbitrary"),
                                             vmem_limit_bytes=VMEM_LIMIT_BYTES),
        name=name,
    )(q_r, gates_g, kc, vc_aug, ks_r, vs_aug, kw_pad, vw_pad, _overlap_t(seq))


def nsa_mixer(proj, gate_logits, cmp_pos, cmp_w1, cmp_w2, cos, sin_signed, *, tag):
    seq = proj.shape[0]
    q_r, ks_r, vs_aug, kw_r, vw_aug = nsa_prep(proj, cos, sin_signed, name="nsa_prep" + tag)
    kc = nsa_compress(proj, KC_OFF, cos, sin_signed, cmp_pos[0], cmp_w1[0], cmp_w2[0],
                      rope=True, aug=False, name="nsa_cmp_k" + tag)
    vc_aug = nsa_compress(proj, VC_OFF, cos, sin_signed, cmp_pos[1], cmp_w1[1], cmp_w2[1],
                          rope=False, aug=True, name="nsa_cmp_v" + tag)
    kw_pad = jnp.pad(kw_r, ((WINDOW, 0), (0, 0)))
    vw_pad = jnp.pad(vw_aug, ((WINDOW, 0), (0, 0)))
    gl = gate_logits.reshape(seq, 3, N_KV, HEADS_PER_KV).transpose(2, 0, 1, 3).reshape(N_KV, seq, 3 * HEADS_PER_KV)
    gl = jnp.pad(gl, ((0, 0), (0, 0), (0, HEAD_DIM - 3 * HEADS_PER_KV)))
    return nsa_attention(q_r, gl, kc, vc_aug, ks_r, vs_aug, kw_pad, vw_pad, name="nsa_attention" + tag)


LANES = 128
SUBLANES = 8


def _exact_dot(x, m):
    return sum(jnp.dot(part, m, preferred_element_type=F32) for part in _split3(x))


def _exact_dot_left(m, x):
    return sum(jnp.dot(m, part, preferred_element_type=F32) for part in _split3(x))


def _ssd_body(z_ref, xbc_ref, dt_ref, convw_ref, convb_ref, dtb_ref, alog_ref, dskip_ref, normw_ref,
              tril_ref, hexp_ref, o_ref, xcat_ref, act_ref, acsx_ref, dtx_ref, y_ref, state_ref):
    c = pl.program_id(0)
    L = SSD_CHUNK
    half = lax.broadcasted_iota(jnp.int32, (L, LANES), 1) < SSM_HEAD_DIM

    @pl.when(c == 0)
    def _():
        xcat_ref[0:SUBLANES, :] = jnp.zeros((SUBLANES, CONV_DIM), F32)
        state_ref[...] = jnp.zeros(state_ref.shape, F32)

    xcat_ref[SUBLANES:SUBLANES + L, :] = xbc_ref[...]
    slab = 512
    for j in range(CONV_DIM // slab):
        cs = slice(j * slab, (j + 1) * slab)
        conv = convb_ref[:, cs] + xcat_ref[pl.ds(SUBLANES - 3, L), cs] * convw_ref[0:1, cs]
        for k in range(1, CONV_WIDTH):
            conv = conv + xcat_ref[pl.ds(SUBLANES - 3 + k, L), cs] * convw_ref[k:k + 1, cs]
        act_ref[:, cs] = conv * jax.nn.sigmoid(conv)
    xcat_ref[0:SUBLANES, :] = xcat_ref[L:L + SUBLANES, :]

    raw = dt_ref[...] + dtb_ref[...]
    dtv = jnp.maximum(raw, 0.0) + jnp.log1p(jnp.exp(-jnp.abs(raw)))
    a = dtv * (-jnp.exp(alog_ref[...]))
    acs = _exact_dot_left(tril_ref[...], a)
    acs_t = acs.T
    hexp = hexp_ref[...]
    acsx_ref[...] = _exact_dot(acs, hexp)
    dtx_ref[...] = _exact_dot(dtv, hexp)

    tri = lax.broadcasted_iota(jnp.int32, (L, L), 0) >= lax.broadcasted_iota(jnp.int32, (L, L), 1)
    bcol = SSM_WIDTH
    ccol = SSM_WIDTH + SSM_GROUPS * SSM_STATE
    for g in range(SSM_GROUPS):
        bg = act_ref[:, bcol + g * SSM_STATE:bcol + (g + 1) * SSM_STATE]
        cg = act_ref[:, ccol + g * SSM_STATE:ccol + (g + 1) * SSM_STATE].astype(BF16)
        cb = _dot_nt(cg, bg.astype(BF16))
        bg_t = bg.T.astype(BF16)
        for jj in range(2):
            j = 2 * g + jj
            h0, h1 = 2 * j, 2 * j + 1
            ts = slice(j * LANES, (j + 1) * LANES)
            xs = act_ref[:, ts]
            col0 = acsx_ref[:, h0 * LANES:(h0 + 1) * LANES]
            col1 = acsx_ref[:, h1 * LANES:(h1 + 1) * LANES]
            acs_pair = jnp.where(half, col0, col1)
            dt_pair = jnp.where(half, dtx_ref[:, h0 * LANES:(h0 + 1) * LANES], dtx_ref[:, h1 * LANES:(h1 + 1) * LANES])
            xdt = xs * dt_pair
            xdt16 = xdt.astype(BF16)
            zero16 = jnp.zeros_like(xdt16)
            seg0 = jnp.concatenate([col0, col0], axis=1) - acs_t[h0:h0 + 1, :]
            seg1 = jnp.concatenate([col1, col1], axis=1) - acs_t[h1:h1 + 1, :]
            g0 = (cb * jnp.exp(jnp.where(tri, seg0, NEG))).astype(BF16)
            g1 = (cb * jnp.exp(jnp.where(tri, seg1, NEG))).astype(BF16)
            y = (jnp.dot(g0, jnp.where(half, xdt16, zero16), preferred_element_type=F32)
                 + jnp.dot(g1, jnp.where(half, zero16, xdt16), preferred_element_type=F32))
            st = state_ref[j]
            y = y + jnp.dot(cg, st.astype(BF16), preferred_element_type=F32) * jnp.exp(acs_pair)
            last = acs_pair[L - 1:L, :]
            xw = (xdt * jnp.exp(last - acs_pair)).astype(BF16)
            state_ref[j] = jnp.exp(last) * st + jnp.dot(bg_t, xw, preferred_element_type=F32)
            y_ref[:, ts] = y + dskip_ref[:, ts] * xs

    gw = SSM_WIDTH // SSM_GROUPS
    for g in range(SSM_GROUPS):
        gs = slice(g * gw, (g + 1) * gw)
        zz = z_ref[:, gs]
        yz = y_ref[:, gs] * (zz * jax.nn.sigmoid(zz))
        ms = jnp.mean(yz * yz, axis=-1, keepdims=True)
        o_ref[:, gs] = (yz * lax.rsqrt(ms + EPS) * normw_ref[:, gs]).astype(o_ref.dtype)


def ssd_mixer(proj, z_col, xbc_col, small, conv_w, conv_b, dt_bias, a_log, d_skip, norm_w, *, name):
    seq = proj.shape[0]
    L = SSD_CHUNK
    pad = lambda v: jnp.pad(v.astype(F32), (0, LANES - v.shape[0])).reshape(1, LANES)
    tril = jnp.asarray(np.tril(np.ones((L, L), np.float32)), BF16)
    hexp = jnp.asarray((np.arange(LANES)[:, None] == np.arange(SSM_HEADS * LANES)[None, :] // LANES)
                       .astype(np.float32), BF16)
    dskip_x = jnp.repeat(d_skip.astype(F32), SSM_HEAD_DIM).reshape(1, SSM_WIDTH)
    full = lambda shape: pl.BlockSpec(shape, lambda c: (0,) * len(shape))
    return pl.pallas_call(
        _ssd_body,
        grid=(seq // L,),
        in_specs=[pl.BlockSpec((L, SSM_WIDTH), lambda c, o=z_col // SSM_WIDTH: (c, o)),
                  pl.BlockSpec((L, CONV_DIM), lambda c, o=xbc_col // CONV_DIM: (c, o)),
                  pl.BlockSpec((L, LANES), lambda c: (c, 0)),
                  full((CONV_WIDTH, CONV_DIM)), full((1, CONV_DIM)), full((1, LANES)), full((1, LANES)),
                  full((1, SSM_WIDTH)), full((1, SSM_WIDTH)), full((L, L)), full((LANES, SSM_HEADS * LANES))],
        out_specs=pl.BlockSpec((L, SSM_WIDTH), lambda c: (c, 0)),
        out_shape=jax.ShapeDtypeStruct((seq, SSM_WIDTH), BF16),
        scratch_shapes=[pltpu.VMEM((L + SUBLANES, CONV_DIM), F32),
                        pltpu.VMEM((L, CONV_DIM), F32),
                        pltpu.VMEM((L, SSM_HEADS * LANES), F32),
                        pltpu.VMEM((L, SSM_HEADS * LANES), F32),
                        pltpu.VMEM((L, SSM_WIDTH), F32),
                        pltpu.VMEM((SSM_HEADS // 2, SSM_STATE, LANES), F32)],
        compiler_params=pltpu.CompilerParams(dimension_semantics=("arbitrary",),
                                             vmem_limit_bytes=VMEM_LIMIT_BYTES),
        name=name,
    )(proj, proj, small, conv_w, conv_b.reshape(1, CONV_DIM), pad(dt_bias), pad(a_log), dskip_x,
      norm_w.reshape(1, SSM_WIDTH).astype(F32), tril, hexp)


def _rope_tables(seq):
    inv = 1.0 / (ROPE_THETA ** (jnp.arange(0, HEAD_DIM, 2, dtype=F32) / HEAD_DIM))
    ang = jnp.arange(seq, dtype=F32)[:, None] * inv[None, :]
    ang = jnp.concatenate([ang, ang], axis=-1)
    return jnp.cos(ang), jnp.sin(ang)


def kernel(x, norm_mix, w_in, cmp_pos, cmp_w1, cmp_w2, attn_norm, conv_w, conv_b, dt_bias, a_log,
           d_skip, ssm_norm, w_out, norm_mlp, w_up, w_down, norm_final):
    batch, seq, _ = x.shape
    assert batch == 1
    cos, sin = _rope_tables(seq)
    half_sign = jnp.concatenate([-jnp.ones((HEAD_DIM // 2,), F32), jnp.ones((HEAD_DIM // 2,), F32)])
    sin_signed = sin * half_sign
    h = x.reshape(seq, D_MODEL)
    for l in range(DEPTH):
        w_big = jnp.concatenate([w_in[l, :, :ATTN_WIDTH], w_in[l, :, W_Z_OFF:W_DT_OFF],
                                 w_in[l, :, W_KV_OFF:W_GATE_OFF]], axis=1).astype(BF16)
        w_small = jnp.concatenate(
            [w_in[l, :, W_DT_OFF:], w_in[l, :, W_GATE_OFF:W_Z_OFF],
             jnp.zeros((D_MODEL, SMALL_COLS - SSM_HEADS - GATE_COLS), F32)], axis=1).astype(BF16)
        xn = rmsnorm(h, norm_mix[l], out_dtype=BF16, name=f"norm_mix{l}")
        proj = matmul(xn, w_big, name=f"in_proj{l}")
        small = matmul(xn, w_small, name=f"in_proj_small{l}")
        attn = nsa_mixer(proj, small[:, SSM_HEADS:SSM_HEADS + GATE_COLS], cmp_pos[l], cmp_w1[l], cmp_w2[l],
                         cos, sin_signed, tag=str(l))
        ssm = ssd_mixer(proj, Z_COL, XBC_COL, small, conv_w[l], conv_b[l], dt_bias[l], a_log[l], d_skip[l],
                        ssm_norm[l], name=f"ssd{l}")
        mix = norm_concat(attn, attn_norm[l], ssm, name=f"norm_concat{l}")
        h = matmul(mix, w_out[l].astype(BF16), residual=h, bn=512, name=f"out_proj{l}")
        hn = rmsnorm(h, norm_mlp[l], out_dtype=BF16, name=f"norm_mlp{l}")
        u = matmul(hn, w_up[l].astype(BF16), act="relu2", out_dtype=BF16, name=f"mlp_up{l}")
        h = matmul(u, w_down[l].astype(BF16), residual=h, bk=2048, name=f"mlp_down{l}")
    out = rmsnorm(h, norm_final, out_dtype=F32, name="norm_final")
    return out.reshape(batch, seq, D_MODEL)
```

```python
import functools

import jax
import jax.numpy as jnp
import numpy as np
from jax import lax
from jax.experimental import pallas as pl
from jax.experimental.pallas import tpu as pltpu

F32 = jnp.float32
BF16 = jnp.bfloat16

D_MODEL = 4096
DEPTH = 2
ATTN_WIDTH = 2048
SSM_WIDTH = 2048
HEAD_DIM = 128
N_HEADS = 16
N_KV = 4
HEADS_PER_KV = 4
KV_WIDTH = 512
ROPE_THETA = 10000.0
CMP_LEN = 32
CMP_STRIDE = 16
SEL_BLOCK = 64
SEL_TOP = 16
WINDOW = 512
Q_BLOCK = 128
BIG = 1e9
SSM_HEAD_DIM = 64
SSM_HEADS = 32
SSM_GROUPS = 8
SSM_STATE = 128
CONV_WIDTH = 4
SSD_CHUNK = 256
CONV_DIM = SSM_WIDTH + 2 * SSM_GROUPS * SSM_STATE
D_FF = 4 * D_MODEL
EPS = 1e-5
NEG = -1e30

VMEM_LIMIT_BYTES = 52 * 1024 * 1024

GATE_COLS = 3 * N_HEADS
W_KV_OFF = ATTN_WIDTH
W_GATE_OFF = W_KV_OFF + 6 * KV_WIDTH
W_Z_OFF = W_GATE_OFF + GATE_COLS
W_XBC_OFF = W_Z_OFF + SSM_WIDTH
W_DT_OFF = W_XBC_OFF + CONV_DIM
Z_COL = ATTN_WIDTH
XBC_COL = Z_COL + SSM_WIDTH
KC_OFF = XBC_COL + CONV_DIM
VC_OFF, KS_OFF, VS_OFF, KW_OFF, VW_OFF = (KC_OFF + i * KV_WIDTH for i in range(1, 6))
SMALL_COLS = 128


def _matmul_body(*refs, nk, act, has_res):
    a_ref, b_ref = refs[0], refs[1]
    r_ref = refs[2] if has_res else None
    o_ref = refs[3] if has_res else refs[2]
    part = jnp.dot(a_ref[...], b_ref[...], preferred_element_type=F32)

    def finish(acc):
        if act == "relu2":
            acc = jnp.square(jnp.maximum(acc, 0.0))
        if has_res:
            acc = acc + r_ref[...]
        o_ref[...] = acc.astype(o_ref.dtype)

    if nk == 1:
        finish(part)
    else:
        acc_ref = refs[-1]
        k = pl.program_id(2)

        @pl.when(k == 0)
        def _():
            acc_ref[...] = part

        @pl.when(k > 0)
        def _():
            acc_ref[...] += part

        @pl.when(k == nk - 1)
        def _():
            finish(acc_ref[...])


def matmul(a, b, layer, *, residual=None, act=None, out_dtype=F32, bm=1024, bn=1024, bk=None, name):
    m, kdim = a.shape
    _, _, n = b.shape
    bk = kdim if bk is None else bk
    bn = min(bn, n)
    assert m % bm == 0 and n % bn == 0 and kdim % bk == 0
    nk = kdim // bk
    in_specs = [pl.BlockSpec((bm, bk), lambda i, j, k: (i, k)),
                pl.BlockSpec((None, bk, bn), lambda i, j, k: (layer, k, j))]
    args = [a, b]
    if residual is not None:
        in_specs.append(pl.BlockSpec((bm, bn), lambda i, j, k: (i, j)))
        args.append(residual)
    return pl.pallas_call(
        functools.partial(_matmul_body, nk=nk, act=act, has_res=residual is not None),
        grid=(m // bm, n // bn, nk),
        in_specs=in_specs,
        out_specs=pl.BlockSpec((bm, bn), lambda i, j, k: (i, j)),
        out_shape=jax.ShapeDtypeStruct((m, n), out_dtype),
        scratch_shapes=[pltpu.VMEM((bm, bn), F32)] if nk > 1 else [],
        compiler_params=pltpu.CompilerParams(
            dimension_semantics=("parallel", "parallel", "arbitrary"),
            vmem_limit_bytes=VMEM_LIMIT_BYTES),
        name=name,
    )(*args)


def _rmsnorm_body(x_ref, w_ref, o_ref):
    x = x_ref[...]
    y = x * lax.rsqrt(jnp.mean(x * x, axis=-1, keepdims=True) + EPS)
    o_ref[...] = (y * w_ref[...]).astype(o_ref.dtype)


def rmsnorm(x, w, *, out_dtype, rows=256, name):
    m, d = x.shape
    return pl.pallas_call(
        _rmsnorm_body,
        grid=(m // rows,),
        in_specs=[pl.BlockSpec((rows, d), lambda i: (i, 0)),
                  pl.BlockSpec((1, d), lambda i: (0, 0))],
        out_specs=pl.BlockSpec((rows, d), lambda i: (i, 0)),
        out_shape=jax.ShapeDtypeStruct((m, d), out_dtype),
        compiler_params=pltpu.CompilerParams(dimension_semantics=("parallel",)),
        name=name,
    )(x, w.reshape(1, d))


def _norm_concat_body(a_ref, w_ref, s_ref, o_ref):
    x = a_ref[...]
    y = x * lax.rsqrt(jnp.mean(x * x, axis=-1, keepdims=True) + EPS)
    o_ref[:, :ATTN_WIDTH] = (y * w_ref[...]).astype(o_ref.dtype)
    o_ref[:, ATTN_WIDTH:] = s_ref[...]


def norm_concat(attn, w, ssm, *, rows=256, name):
    m = attn.shape[0]
    return pl.pallas_call(
        _norm_concat_body,
        grid=(m // rows,),
        in_specs=[pl.BlockSpec((rows, ATTN_WIDTH), lambda i: (i, 0)),
                  pl.BlockSpec((1, ATTN_WIDTH), lambda i: (0, 0)),
                  pl.BlockSpec((rows, SSM_WIDTH), lambda i: (i, 0))],
        out_specs=pl.BlockSpec((rows, D_MODEL), lambda i: (i, 0)),
        out_shape=jax.ShapeDtypeStruct((m, D_MODEL), BF16),
        compiler_params=pltpu.CompilerParams(dimension_semantics=("parallel",)),
        name=name,
    )(attn, w.reshape(1, ATTN_WIDTH), ssm)


SEL_CHUNK = 1024


def _split3(x):
    hi = x.astype(BF16)
    r1 = x - hi.astype(F32)
    mid = r1.astype(BF16)
    lo = (r1 - mid.astype(F32)).astype(BF16)
    return hi, mid, lo


def _dot_nt(a, b):
    return lax.dot_general(a, b, (((1,), (1,)), ((), ())), preferred_element_type=F32)


def _prep_body(q_ref, ks_ref, vs_ref, kw_ref, vw_ref, cos_ref, sin_ref, qo, kso, vso, kwo, vwo):
    cos = cos_ref[...]
    sin = sin_ref[...]
    scale = HEAD_DIM ** -0.5

    def rope(t):
        return t * cos + pltpu.roll(t, HEAD_DIM // 2, 1) * sin

    for h in range(N_HEADS):
        sl = slice(h * HEAD_DIM, (h + 1) * HEAD_DIM)
        qo[:, sl] = (rope(q_ref[:, sl]) * scale).astype(BF16)
    ones = jnp.ones((q_ref.shape[0], HEAD_DIM), BF16)
    for g in range(N_KV):
        sl = slice(g * HEAD_DIM, (g + 1) * HEAD_DIM)
        kso[:, sl] = rope(ks_ref[:, sl]).astype(BF16)
        kwo[:, sl] = rope(kw_ref[:, sl]).astype(BF16)
        vso[:, 2 * g * HEAD_DIM:(2 * g + 1) * HEAD_DIM] = vs_ref[:, sl].astype(BF16)
        vso[:, (2 * g + 1) * HEAD_DIM:(2 * g + 2) * HEAD_DIM] = ones
        vwo[:, 2 * g * HEAD_DIM:(2 * g + 1) * HEAD_DIM] = vw_ref[:, sl].astype(BF16)
        vwo[:, (2 * g + 1) * HEAD_DIM:(2 * g + 2) * HEAD_DIM] = ones


def nsa_prep(proj, cos, sin_signed, *, rows=512, name):
    seq = proj.shape[0]
    kvb = lambda off: pl.BlockSpec((rows, KV_WIDTH), lambda i, o=off // KV_WIDTH: (i, o))
    return pl.pallas_call(
        _prep_body,
        grid=(seq // rows,),
        in_specs=[pl.BlockSpec((rows, ATTN_WIDTH), lambda i: (i, 0)),
                  kvb(KS_OFF), kvb(VS_OFF), kvb(KW_OFF), kvb(VW_OFF),
                  pl.BlockSpec((rows, HEAD_DIM), lambda i: (i, 0)),
                  pl.BlockSpec((rows, HEAD_DIM), lambda i: (i, 0))],
        out_specs=[pl.BlockSpec((rows, ATTN_WIDTH), lambda i: (i, 0)),
                   pl.BlockSpec((rows, KV_WIDTH), lambda i: (i, 0)),
                   pl.BlockSpec((rows, 2 * KV_WIDTH), lambda i: (i, 0)),
                   pl.BlockSpec((rows, KV_WIDTH), lambda i: (i, 0)),
                   pl.BlockSpec((rows, 2 * KV_WIDTH), lambda i: (i, 0))],
        out_shape=[jax.ShapeDtypeStruct((seq, ATTN_WIDTH), BF16),
                   jax.ShapeDtypeStruct((seq, KV_WIDTH), BF16),
                   jax.ShapeDtypeStruct((seq, 2 * KV_WIDTH), BF16),
                   jax.ShapeDtypeStruct((seq, KV_WIDTH), BF16),
                   jax.ShapeDtypeStruct((seq, 2 * KV_WIDTH), BF16)],
        compiler_params=pltpu.CompilerParams(dimension_semantics=("parallel",),
                                             vmem_limit_bytes=VMEM_LIMIT_BYTES),
        name=name,
    )(proj, proj, proj, proj, proj, cos, sin_signed)


def _compress_body(t_ref, cos_ref, sin_ref, pos_ref, w1_ref, w2_ref, o_ref, tr_ref, *, rope, aug):
    seq = t_ref.shape[0]
    ncp = seq // CMP_STRIDE
    t = t_ref[...]
    if rope:
        t = t * cos_ref[...] + pltpu.roll(t, HEAD_DIM // 2, 1) * sin_ref[...]
    tr_ref[...] = t
    acc_a = jnp.zeros((ncp, HEAD_DIM), F32)
    acc_b = jnp.zeros((ncp, HEAD_DIM), F32)
    for p in range(CMP_STRIDE):
        tp = tr_ref[pl.ds(p, ncp, stride=CMP_STRIDE), :]
        wa = w1_ref[p * HEAD_DIM:(p + 1) * HEAD_DIM, :].astype(BF16)
        wb = w1_ref[(CMP_STRIDE + p) * HEAD_DIM:(CMP_STRIDE + p + 1) * HEAD_DIM, :].astype(BF16)
        acc_a += jnp.dot((tp + pos_ref[p:p + 1, :]).astype(BF16), wa, preferred_element_type=F32)
        acc_b += jnp.dot((tp + pos_ref[CMP_STRIDE + p:CMP_STRIDE + p + 1, :]).astype(BF16), wb,
                         preferred_element_type=F32)
    pre = acc_a + pltpu.roll(acc_b, ncp - 1, 0)
    hid = pre * jax.nn.sigmoid(pre)
    out = jnp.dot(hid.astype(BF16), w2_ref[...].astype(BF16), preferred_element_type=F32)
    o_ref[0, :, :HEAD_DIM] = out.astype(BF16)
    if aug:
        o_ref[0, :, HEAD_DIM:] = jnp.ones((ncp, HEAD_DIM), BF16)


def nsa_compress(proj, col_off, cos, sin_signed, pos, w1, w2, *, rope, aug, name):
    seq = proj.shape[0]
    ncp = seq // CMP_STRIDE
    width = 2 * HEAD_DIM if aug else HEAD_DIM
    return pl.pallas_call(
        functools.partial(_compress_body, rope=rope, aug=aug),
        grid=(N_KV,),
        in_specs=[pl.BlockSpec((seq, HEAD_DIM), lambda g, o=col_off // HEAD_DIM: (0, o + g)),
                  pl.BlockSpec((seq, HEAD_DIM), lambda g: (0, 0)),
                  pl.BlockSpec((seq, HEAD_DIM), lambda g: (0, 0)),
                  pl.BlockSpec((CMP_LEN, HEAD_DIM), lambda g: (0, 0)),
                  pl.BlockSpec((CMP_LEN * HEAD_DIM, HEAD_DIM), lambda g: (0, 0)),
                  pl.BlockSpec((HEAD_DIM, HEAD_DIM), lambda g: (0, 0))],
        out_specs=pl.BlockSpec((1, ncp, width), lambda g: (g, 0, 0)),
        out_shape=jax.ShapeDtypeStruct((N_KV, ncp, width), BF16),
        scratch_shapes=[pltpu.VMEM((seq, HEAD_DIM), F32)],
        compiler_params=pltpu.CompilerParams(dimension_semantics=("parallel",),
                                             vmem_limit_bytes=VMEM_LIMIT_BYTES),
        name=name,
    )(proj, cos, sin_signed, pos, w1, w2)


def _nsa_body(q_ref, gate_ref, kc_ref, vc_ref, ks_ref, vs_ref, kw_ref, vw_ref, ovt_ref, o_ref,
              sc_ref, m_ref, acc_ref, *, seq):
    i = pl.program_id(1)
    t0 = i * Q_BLOCK
    ncp = seq // CMP_STRIDE
    nsel = seq // SEL_BLOCK
    top = min(SEL_TOP, nsel)
    nwin = WINDOW + Q_BLOCK
    rows = [slice(r * Q_BLOCK, (r + 1) * Q_BLOCK) for r in range(HEADS_PER_KV)]

    q4 = jnp.concatenate([q_ref[:, r * HEAD_DIM:(r + 1) * HEAD_DIM] for r in range(HEADS_PER_KV)], axis=0)

    s_c = _dot_nt(q4, kc_ref[0])
    c_end = lax.broadcasted_iota(jnp.int32, (Q_BLOCK, ncp), 1) * CMP_STRIDE + (CMP_LEN - 1)
    tq_c = t0 + lax.broadcasted_iota(jnp.int32, (Q_BLOCK, ncp), 0)
    cmask = c_end <= tq_c
    vc = vc_ref[0]
    psum = jnp.zeros((Q_BLOCK, ncp), F32)
    o_cmp = []
    for r in range(HEADS_PER_KV):
        s = jnp.where(cmask, s_c[rows[r]], NEG)
        m = jnp.max(s, axis=-1, keepdims=True)
        p = jnp.where(cmask, jnp.exp(s - m), 0.0)
        pn = p / jnp.maximum(jnp.sum(p, axis=-1, keepdims=True), 1e-30)
        psum = psum + pn
        o_cmp.append(jnp.dot(pn.astype(BF16), vc[:, :HEAD_DIM], preferred_element_type=F32))

    ovt = ovt_ref[...]
    imp = sum(_dot_nt(ovt, part) for part in _split3(psum))
    blk = lax.broadcasted_iota(jnp.int32, (nsel, Q_BLOCK), 0)
    cur = (t0 + lax.broadcasted_iota(jnp.int32, (nsel, Q_BLOCK), 1)) // SEL_BLOCK
    blkf = blk.astype(F32)
    imp = jnp.where(blk == 0, BIG, jnp.where(blk == cur, BIG, jnp.where(blk == cur - 1, BIG, imp)))
    imp = jnp.where(blk > cur, -BIG, imp)
    sel_t = jnp.zeros((nsel, Q_BLOCK), F32)
    for _ in range(top):
        mx = jnp.max(imp, axis=0, keepdims=True)
        first = jnp.min(jnp.where(imp == mx, blkf, float(nsel)), axis=0, keepdims=True)
        hit = blkf == first
        sel_t = jnp.where(hit, 1.0, sel_t)
        imp = jnp.where(hit, -jnp.inf, imp)
    sel_t = jnp.where(blk <= cur, sel_t, 0.0)
    sel = sel_t.T.astype(BF16)

    n_chunks = (t0 + Q_BLOCK + SEL_CHUNK - 1) // SEL_CHUNK
    diff = (lax.broadcasted_iota(jnp.int32, (nsel, SEL_CHUNK), 0)
            - lax.broadcasted_iota(jnp.int32, (nsel, SEL_CHUNK), 1) // SEL_BLOCK)
    lane_k = lax.broadcasted_iota(jnp.int32, (Q_BLOCK, SEL_CHUNK), 1)
    tq_s = t0 + lax.broadcasted_iota(jnp.int32, (Q_BLOCK, SEL_CHUNK), 0)
    lane_tiles = SEL_CHUNK // HEAD_DIM
    m_ref[...] = jnp.full(m_ref.shape, NEG, F32)

    def pass1(c, carry):
        k0 = pl.multiple_of(c * SEL_CHUNK, SEL_CHUNK)
        s = _dot_nt(q4, ks_ref[pl.ds(k0, SEL_CHUNK), :])
        expand = jnp.where(diff == c * (SEL_CHUNK // SEL_BLOCK), 1.0, 0.0).astype(BF16)
        picked = jnp.dot(sel, expand, preferred_element_type=F32)
        picked = jnp.where(k0 + lane_k <= tq_s, picked, 0.0)
        mask = picked > 0.5
        for r in range(HEADS_PER_KV):
            sm = jnp.where(mask, s[rows[r]], NEG)
            sc_ref[c, rows[r], :] = sm
            best = m_ref[rows[r], :]
            for t in range(lane_tiles):
                best = jnp.maximum(best, sm[:, t * HEAD_DIM:(t + 1) * HEAD_DIM])
            m_ref[rows[r], :] = best
        return carry

    lax.fori_loop(0, n_chunks, pass1, 0)
    m_ref[...] = jnp.broadcast_to(jnp.max(m_ref[...], axis=-1, keepdims=True), m_ref.shape)

    acc_ref[...] = jnp.zeros(acc_ref.shape, F32)

    def pass2(c, carry):
        k0 = pl.multiple_of(c * SEL_CHUNK, SEL_CHUNK)
        p = jnp.exp(sc_ref[c] - jnp.concatenate([m_ref[...]] * lane_tiles, axis=1)).astype(BF16)
        acc_ref[...] += jnp.dot(p, vs_ref[pl.ds(k0, SEL_CHUNK), :], preferred_element_type=F32)
        return carry

    lax.fori_loop(0, n_chunks, pass2, 0)

    w0 = pl.multiple_of(t0, Q_BLOCK)
    s_w = _dot_nt(q4, kw_ref[pl.ds(w0, nwin), :])
    vwin = vw_ref[pl.ds(w0, nwin), :]
    wpos = t0 - WINDOW + lax.broadcasted_iota(jnp.int32, (Q_BLOCK, nwin), 1)
    tq_w = t0 + lax.broadcasted_iota(jnp.int32, (Q_BLOCK, nwin), 0)
    wvalid = jnp.where(wpos <= tq_w, jnp.where(wpos > tq_w - WINDOW, jnp.where(wpos >= 0, 1.0, 0.0), 0.0), 0.0)
    wmask = wvalid > 0.5

    gate = jax.nn.sigmoid(gate_ref[0])
    for r in range(HEADS_PER_KV):
        sm = jnp.where(wmask, s_w[rows[r]], NEG)
        pw = jnp.exp(sm - jnp.max(sm, axis=-1, keepdims=True)).astype(BF16)
        res_w = jnp.dot(pw, vwin, preferred_element_type=F32)
        o_win = res_w[:, :HEAD_DIM] / res_w[:, HEAD_DIM:]
        res_s = acc_ref[rows[r], :]
        o_sel = res_s[:, :HEAD_DIM] / res_s[:, HEAD_DIM:]
        g_c = gate[:, r:r + 1]
        g_s = gate[:, HEADS_PER_KV + r:HEADS_PER_KV + r + 1]
        g_w = gate[:, 2 * HEADS_PER_KV + r:2 * HEADS_PER_KV + r + 1]
        o_ref[:, r * HEAD_DIM:(r + 1) * HEAD_DIM] = g_c * o_cmp[r] + g_s * o_sel + g_w * o_win


def _overlap_t(seq):
    ncp = seq // CMP_STRIDE
    nsel = seq // SEL_BLOCK
    c_start = np.arange(ncp) * CMP_STRIDE
    j_start = np.arange(nsel) * SEL_BLOCK
    ov = ((c_start[None, :] < j_start[:, None] + SEL_BLOCK) & (c_start[None, :] + CMP_LEN > j_start[:, None])
          & (np.arange(ncp)[None, :] < ncp - 1))
    return jnp.asarray(ov.astype(np.float32), BF16)


def nsa_attention(q_r, gates_g, kc, vc_aug, ks_r, vs_aug, kw_pad, vw_pad, *, name):
    seq = q_r.shape[0]
    nq = seq // Q_BLOCK
    ncp = seq // CMP_STRIDE
    grp = HEADS_PER_KV * HEAD_DIM
    return pl.pallas_call(
        functools.partial(_nsa_body, seq=seq),
        grid=(N_KV, nq),
        in_specs=[pl.BlockSpec((Q_BLOCK, grp), lambda g, i: (i, g)),
                  pl.BlockSpec((1, Q_BLOCK, HEAD_DIM), lambda g, i: (g, i, 0)),
                  pl.BlockSpec((1, ncp, HEAD_DIM), lambda g, i: (g, 0, 0)),
                  pl.BlockSpec((1, ncp, 2 * HEAD_DIM), lambda g, i: (g, 0, 0)),
                  pl.BlockSpec((seq, HEAD_DIM), lambda g, i: (0, g)),
                  pl.BlockSpec((seq, 2 * HEAD_DIM), lambda g, i: (0, g)),
                  pl.BlockSpec((seq + WINDOW, HEAD_DIM), lambda g, i: (0, g)),
                  pl.BlockSpec((seq + WINDOW, 2 * HEAD_DIM), lambda g, i: (0, g)),
                  pl.BlockSpec((seq // SEL_BLOCK, ncp), lambda g, i: (0, 0))],
        out_specs=pl.BlockSpec((Q_BLOCK, grp), lambda g, i: (i, g)),
        out_shape=jax.ShapeDtypeStruct((seq, ATTN_WIDTH), F32),
        scratch_shapes=[pltpu.VMEM((seq // SEL_CHUNK, HEADS_PER_KV * Q_BLOCK, SEL_CHUNK), F32),
                        pltpu.VMEM((HEADS_PER_KV * Q_BLOCK, HEAD_DIM), F32),
                        pltpu.VMEM((HEADS_PER_KV * Q_BLOCK, 2 * HEAD_DIM), F32)],
        compiler_params=pltpu.CompilerParams(dimension_semantics=("parallel", "arbitrary"),
                                             vmem_limit_bytes=VMEM_LIMIT_BYTES),
        name=name,
    )(q_r, gates_g, kc, vc_aug, ks_r, vs_aug, kw_pad, vw_pad, _overlap_t(seq))


def nsa_mixer(proj, gate_logits, cmp_pos, cmp_w1, cmp_w2, cos, sin_signed, *, tag):
    seq = proj.shape[0]
    q_r, ks_r, vs_aug, kw_r, vw_aug = nsa_prep(proj, cos, sin_signed, name="nsa_prep" + tag)
    kc = nsa_compress(proj, KC_OFF, cos, sin_signed, cmp_pos[0], cmp_w1[0], cmp_w2[0],
                      rope=True, aug=False, name="nsa_cmp_k" + tag)
    vc_aug = nsa_compress(proj, VC_OFF, cos, sin_signed, cmp_pos[1], cmp_w1[1], cmp_w2[1],
                          rope=False, aug=True, name="nsa_cmp_v" + tag)
    kw_pad = jnp.pad(kw_r, ((WINDOW, 0), (0, 0)))
    vw_pad = jnp.pad(vw_aug, ((WINDOW, 0), (0, 0)))
    gl = gate_logits.reshape(seq, 3, N_KV, HEADS_PER_KV).transpose(2, 0, 1, 3).reshape(N_KV, seq, 3 * HEADS_PER_KV)
    gl = jnp.pad(gl, ((0, 0), (0, 0), (0, HEAD_DIM - 3 * HEADS_PER_KV)))
    return nsa_attention(q_r, gl, kc, vc_aug, ks_r, vs_aug, kw_pad, vw_pad, name="nsa_attention" + tag)


LANES = 128
SUBLANES = 8


def _exact_dot(x, m):
    return sum(jnp.dot(part, m, preferred_element_type=F32) for part in _split3(x))


def _exact_dot_left(m, x):
    return sum(jnp.dot(m, part, preferred_element_type=F32) for part in _split3(x))


def _ssd_body(z_ref, xbc_ref, dt_ref, convw_ref, convb_ref, dtb_ref, alog_ref, dskip_ref, normw_ref,
              tril_ref, hexp_ref, o_ref, xcat_ref, act_ref, acsx_ref, dtx_ref, y_ref, state_ref):
    c = pl.program_id(0)
    L = SSD_CHUNK
    half = lax.broadcasted_iota(jnp.int32, (L, LANES), 1) < SSM_HEAD_DIM

    @pl.when(c == 0)
    def _():
        xcat_ref[0:SUBLANES, :] = jnp.zeros((SUBLANES, CONV_DIM), F32)
        state_ref[...] = jnp.zeros(state_ref.shape, F32)

    xcat_ref[SUBLANES:SUBLANES + L, :] = xbc_ref[...]
    slab = 512
    for j in range(CONV_DIM // slab):
        cs = slice(j * slab, (j + 1) * slab)
        conv = convb_ref[:, cs] + xcat_ref[pl.ds(SUBLANES - 3, L), cs] * convw_ref[0:1, cs]
        for k in range(1, CONV_WIDTH):
            conv = conv + xcat_ref[pl.ds(SUBLANES - 3 + k, L), cs] * convw_ref[k:k + 1, cs]
        act_ref[:, cs] = conv * jax.nn.sigmoid(conv)
    xcat_ref[0:SUBLANES, :] = xcat_ref[L:L + SUBLANES, :]

    raw = dt_ref[...] + dtb_ref[...]
    dtv = jnp.maximum(raw, 0.0) + jnp.log1p(jnp.exp(-jnp.abs(raw)))
    a = dtv * (-jnp.exp(alog_ref[...]))
    acs = _exact_dot_left(tril_ref[...], a)
    acs_t = acs.T
    hexp = hexp_ref[...]
    acsx_ref[...] = _exact_dot(acs, hexp)
    dtx_ref[...] = _exact_dot(dtv, hexp)

    tri = lax.broadcasted_iota(jnp.int32, (L, L), 0) >= lax.broadcasted_iota(jnp.int32, (L, L), 1)
    bcol = SSM_WIDTH
    ccol = SSM_WIDTH + SSM_GROUPS * SSM_STATE
    for g in range(SSM_GROUPS):
        bg = act_ref[:, bcol + g * SSM_STATE:bcol + (g + 1) * SSM_STATE]
        cg = act_ref[:, ccol + g * SSM_STATE:ccol + (g + 1) * SSM_STATE].astype(BF16)
        cb = _dot_nt(cg, bg.astype(BF16))
        bg_t = bg.T.astype(BF16)
        for jj in range(2):
            j = 2 * g + jj
            h0, h1 = 2 * j, 2 * j + 1
            ts = slice(j * LANES, (j + 1) * LANES)
            xs = act_ref[:, ts]
            col0 = acsx_ref[:, h0 * LANES:(h0 + 1) * LANES]
            col1 = acsx_ref[:, h1 * LANES:(h1 + 1) * LANES]
            acs_pair = jnp.where(half, col0, col1)
            dt_pair = jnp.where(half, dtx_ref[:, h0 * LANES:(h0 + 1) * LANES], dtx_ref[:, h1 * LANES:(h1 + 1) * LANES])
            xdt = xs * dt_pair
            xdt16 = xdt.astype(BF16)
            zero16 = jnp.zeros_like(xdt16)
            seg0 = jnp.concatenate([col0, col0], axis=1) - acs_t[h0:h0 + 1, :]
            seg1 = jnp.concatenate([col1, col1], axis=1) - acs_t[h1:h1 + 1, :]
            g0 = (cb * jnp.exp(jnp.where(tri, seg0, NEG))).astype(BF16)
            g1 = (cb * jnp.exp(jnp.where(tri, seg1, NEG))).astype(BF16)
            y = (jnp.dot(g0, jnp.where(half, xdt16, zero16), preferred_element_type=F32)
                 + jnp.dot(g1, jnp.where(half, zero16, xdt16), preferred_element_type=F32))
            st = state_ref[j]
            y = y + jnp.dot(cg, st.astype(BF16), preferred_element_type=F32) * jnp.exp(acs_pair)
            last = acs_pair[L - 1:L, :]
            xw = (xdt * jnp.exp(last - acs_pair)).astype(BF16)
            state_ref[j] = jnp.exp(last) * st + jnp.dot(bg_t, xw, preferred_element_type=F32)
            y_ref[:, ts] = y + dskip_ref[:, ts] * xs

    gw = SSM_WIDTH // SSM_GROUPS
    for g in range(SSM_GROUPS):
        gs = slice(g * gw, (g + 1) * gw)
        zz = z_ref[:, gs]
        yz = y_ref[:, gs] * (zz * jax.nn.sigmoid(zz))
        ms = jnp.mean(yz * yz, axis=-1, keepdims=True)
        o_ref[:, gs] = (yz * lax.rsqrt(ms + EPS) * normw_ref[:, gs]).astype(o_ref.dtype)


def ssd_mixer(proj, z_col, xbc_col, small, conv_w, conv_b, dt_bias, a_log, d_skip, norm_w, *, name):
    seq = proj.shape[0]
    L = SSD_CHUNK
    pad = lambda v: jnp.pad(v.astype(F32), (0, LANES - v.shape[0])).reshape(1, LANES)
    tril = jnp.asarray(np.tril(np.ones((L, L), np.float32)), BF16)
    hexp = jnp.asarray((np.arange(LANES)[:, None] == np.arange(SSM_HEADS * LANES)[None, :] // LANES)
                       .astype(np.float32), BF16)
    dskip_x = jnp.repeat(d_skip.astype(F32), SSM_HEAD_DIM).reshape(1, SSM_WIDTH)
    full = lambda shape: pl.BlockSpec(shape, lambda c: (0,) * len(shape))
    return pl.pallas_call(
        _ssd_body,
        grid=(seq // L,),
        in_specs=[pl.BlockSpec((L, SSM_WIDTH), lambda c, o=z_col // SSM_WIDTH: (c, o)),
                  pl.BlockSpec((L, CONV_DIM), lambda c, o=xbc_col // CONV_DIM: (c, o)),
                  pl.BlockSpec((L, LANES), lambda c: (c, 0)),
                  full((CONV_WIDTH, CONV_DIM)), full((1, CONV_DIM)), full((1, LANES)), full((1, LANES)),
                  full((1, SSM_WIDTH)), full((1, SSM_WIDTH)), full((L, L)), full((LANES, SSM_HEADS * LANES))],
        out_specs=pl.BlockSpec((L, SSM_WIDTH), lambda c: (c, 0)),
        out_shape=jax.ShapeDtypeStruct((seq, SSM_WIDTH), BF16),
        scratch_shapes=[pltpu.VMEM((L + SUBLANES, CONV_DIM), F32),
                        pltpu.VMEM((L, CONV_DIM), F32),
                        pltpu.VMEM((L, SSM_HEADS * LANES), F32),
                        pltpu.VMEM((L, SSM_HEADS * LANES), F32),
                        pltpu.VMEM((L, SSM_WIDTH), F32),
                        pltpu.VMEM((SSM_HEADS // 2, SSM_STATE, LANES), F32)],
        compiler_params=pltpu.CompilerParams(dimension_semantics=("arbitrary",),
                                             vmem_limit_bytes=VMEM_LIMIT_BYTES),
        name=name,
    )(proj, proj, small, conv_w, conv_b.reshape(1, CONV_DIM), pad(dt_bias), pad(a_log), dskip_x,
      norm_w.reshape(1, SSM_WIDTH).astype(F32), tril, hexp)


def _rope_tables(seq):
    inv = 1.0 / (ROPE_THETA ** (jnp.arange(0, HEAD_DIM, 2, dtype=F32) / HEAD_DIM))
    ang = jnp.arange(seq, dtype=F32)[:, None] * inv[None, :]
    ang = jnp.concatenate([ang, ang], axis=-1)
    return jnp.cos(ang), jnp.sin(ang)


def kernel(x, norm_mix, w_in, cmp_pos, cmp_w1, cmp_w2, attn_norm, conv_w, conv_b, dt_bias, a_log,
           d_skip, ssm_norm, w_out, norm_mlp, w_up, w_down, norm_final):
    batch, seq, _ = x.shape
    assert batch == 1
    cos, sin = _rope_tables(seq)
    half_sign = jnp.concatenate([-jnp.ones((HEAD_DIM // 2,), F32), jnp.ones((HEAD_DIM // 2,), F32)])
    sin_signed = sin * half_sign
    h = x.reshape(seq, D_MODEL)
    w_big = jnp.concatenate([w_in[:, :, :ATTN_WIDTH], w_in[:, :, W_Z_OFF:W_DT_OFF],
                             w_in[:, :, W_KV_OFF:W_GATE_OFF]], axis=2).astype(BF16)
    w_small = jnp.concatenate(
        [w_in[:, :, W_DT_OFF:], w_in[:, :, W_GATE_OFF:W_Z_OFF],
         jnp.zeros((DEPTH, D_MODEL, SMALL_COLS - SSM_HEADS - GATE_COLS), F32)], axis=2).astype(BF16)
    w_out16, w_up16, w_down16 = w_out.astype(BF16), w_up.astype(BF16), w_down.astype(BF16)
    for l in range(DEPTH):
        xn = rmsnorm(h, norm_mix[l], out_dtype=BF16, name=f"norm_mix{l}")
        proj = matmul(xn, w_big, l, name=f"in_proj{l}")
        small = matmul(xn, w_small, l, name=f"in_proj_small{l}")
        attn = nsa_mixer(proj, small[:, SSM_HEADS:SSM_HEADS + GATE_COLS], cmp_pos[l], cmp_w1[l], cmp_w2[l],
                         cos, sin_signed, tag=str(l))
        ssm = ssd_mixer(proj, Z_COL, XBC_COL, small, conv_w[l], conv_b[l], dt_bias[l], a_log[l], d_skip[l],
                        ssm_norm[l], name=f"ssd{l}")
        mix = norm_concat(attn, attn_norm[l], ssm, name=f"norm_concat{l}")
        h = matmul(mix, w_out16, l, residual=h, bn=512, name=f"out_proj{l}")
        hn = rmsnorm(h, norm_mlp[l], out_dtype=BF16, name=f"norm_mlp{l}")
        u = matmul(hn, w_up16, l, act="relu2", out_dtype=BF16, name=f"mlp_up{l}")
        h = matmul(u, w_down16, l, residual=h, bk=2048, name=f"mlp_down{l}")
    out = rmsnorm(h, norm_final, out_dtype=F32, name="norm_final")
    return out.reshape(batch, seq, D_MODEL)
```

```python
import functools

import jax
import jax.numpy as jnp
import numpy as np
from jax import lax
from jax.experimental import pallas as pl
from jax.experimental.pallas import tpu as pltpu

F32 = jnp.float32
BF16 = jnp.bfloat16

D_MODEL = 4096
DEPTH = 2
ATTN_WIDTH = 2048
SSM_WIDTH = 2048
HEAD_DIM = 128
N_HEADS = 16
N_KV = 4
HEADS_PER_KV = 4
KV_WIDTH = 512
ROPE_THETA = 10000.0
CMP_LEN = 32
CMP_STRIDE = 16
SEL_BLOCK = 64
SEL_TOP = 16
WINDOW = 512
Q_BLOCK = 128
BIG = 1e9
SSM_HEAD_DIM = 64
SSM_HEADS = 32
SSM_GROUPS = 8
SSM_STATE = 128
CONV_WIDTH = 4
SSD_CHUNK = 256
CONV_DIM = SSM_WIDTH + 2 * SSM_GROUPS * SSM_STATE
D_FF = 4 * D_MODEL
EPS = 1e-5
NEG = -1e30

VMEM_LIMIT_BYTES = 52 * 1024 * 1024

GATE_COLS = 3 * N_HEADS
W_KV_OFF = ATTN_WIDTH
W_GATE_OFF = W_KV_OFF + 6 * KV_WIDTH
W_Z_OFF = W_GATE_OFF + GATE_COLS
W_XBC_OFF = W_Z_OFF + SSM_WIDTH
W_DT_OFF = W_XBC_OFF + CONV_DIM
Z_COL = ATTN_WIDTH
XBC_COL = Z_COL + SSM_WIDTH
KC_OFF = XBC_COL + CONV_DIM
VC_OFF, KS_OFF, VS_OFF, KW_OFF, VW_OFF = (KC_OFF + i * KV_WIDTH for i in range(1, 6))
SMALL_COLS = 128


def _matmul_body(*refs, nk, act, has_res):
    a_ref, b_ref = refs[0], refs[1]
    r_ref = refs[2] if has_res else None
    o_ref = refs[3] if has_res else refs[2]
    part = jnp.dot(a_ref[...], b_ref[...].astype(BF16), preferred_element_type=F32)

    def finish(acc):
        if act == "relu2":
            acc = jnp.square(jnp.maximum(acc, 0.0))
        if has_res:
            acc = acc + r_ref[...]
        o_ref[...] = acc.astype(o_ref.dtype)

    if nk == 1:
        finish(part)
    else:
        acc_ref = refs[-1]
        k = pl.program_id(2)

        @pl.when(k == 0)
        def _():
            acc_ref[...] = part

        @pl.when(k > 0)
        def _():
            acc_ref[...] += part

        @pl.when(k == nk - 1)
        def _():
            finish(acc_ref[...])


def matmul(a, b, layer, *, residual=None, act=None, out_dtype=F32, bm=1024, bn=1024, bk=None, name):
    m, kdim = a.shape
    _, _, n = b.shape
    bk = kdim if bk is None else bk
    bn = min(bn, n)
    assert m % bm == 0 and n % bn == 0 and kdim % bk == 0
    nk = kdim // bk
    in_specs = [pl.BlockSpec((bm, bk), lambda i, j, k: (i, k)),
                pl.BlockSpec((None, bk, bn), lambda i, j, k: (layer, k, j))]
    args = [a, b]
    if residual is not None:
        in_specs.append(pl.BlockSpec((bm, bn), lambda i, j, k: (i, j)))
        args.append(residual)
    return pl.pallas_call(
        functools.partial(_matmul_body, nk=nk, act=act, has_res=residual is not None),
        grid=(m // bm, n // bn, nk),
        in_specs=in_specs,
        out_specs=pl.BlockSpec((bm, bn), lambda i, j, k: (i, j)),
        out_shape=jax.ShapeDtypeStruct((m, n), out_dtype),
        scratch_shapes=[pltpu.VMEM((bm, bn), F32)] if nk > 1 else [],
        compiler_params=pltpu.CompilerParams(
            dimension_semantics=("parallel", "parallel", "arbitrary"),
            vmem_limit_bytes=VMEM_LIMIT_BYTES),
        name=name,
    )(*args)


def _rmsnorm_body(x_ref, w_ref, o_ref):
    x = x_ref[...]
    y = x * lax.rsqrt(jnp.mean(x * x, axis=-1, keepdims=True) + EPS)
    o_ref[...] = (y * w_ref[...]).astype(o_ref.dtype)


def rmsnorm(x, w, *, out_dtype, rows=256, name):
    m, d = x.shape
    return pl.pallas_call(
        _rmsnorm_body,
        grid=(m // rows,),
        in_specs=[pl.BlockSpec((rows, d), lambda i: (i, 0)),
                  pl.BlockSpec((1, d), lambda i: (0, 0))],
        out_specs=pl.BlockSpec((rows, d), lambda i: (i, 0)),
        out_shape=jax.ShapeDtypeStruct((m, d), out_dtype),
        compiler_params=pltpu.CompilerParams(dimension_semantics=("parallel",)),
        name=name,
    )(x, w.reshape(1, d))


def _norm_concat_body(a_ref, w_ref, s_ref, o_ref):
    x = a_ref[...]
    y = x * lax.rsqrt(jnp.mean(x * x, axis=-1, keepdims=True) + EPS)
    o_ref[:, :ATTN_WIDTH] = (y * w_ref[...]).astype(o_ref.dtype)
    o_ref[:, ATTN_WIDTH:] = s_ref[...]


def norm_concat(attn, w, ssm, *, rows=256, name):
    m = attn.shape[0]
    return pl.pallas_call(
        _norm_concat_body,
        grid=(m // rows,),
        in_specs=[pl.BlockSpec((rows, ATTN_WIDTH), lambda i: (i, 0)),
                  pl.BlockSpec((1, ATTN_WIDTH), lambda i: (0, 0)),
                  pl.BlockSpec((rows, SSM_WIDTH), lambda i: (i, 0))],
        out_specs=pl.BlockSpec((rows, D_MODEL), lambda i: (i, 0)),
        out_shape=jax.ShapeDtypeStruct((m, D_MODEL), BF16),
        compiler_params=pltpu.CompilerParams(dimension_semantics=("parallel",)),
        name=name,
    )(attn, w.reshape(1, ATTN_WIDTH), ssm)


SEL_CHUNK = 1024
SEL_SUB = 256
LOG2_E = 1.4426950408889634


def _split3(x):
    hi = x.astype(BF16)
    r1 = x - hi.astype(F32)
    mid = r1.astype(BF16)
    lo = (r1 - mid.astype(F32)).astype(BF16)
    return hi, mid, lo


def _dot_nt(a, b):
    return lax.dot_general(a, b, (((1,), (1,)), ((), ())), preferred_element_type=F32)


def _prep_body(q_ref, ks_ref, vs_ref, kw_ref, vw_ref, cos_ref, sin_ref, qo, kso, vso, kwo, vwo):
    cos = cos_ref[...]
    sin = sin_ref[...]
    scale = HEAD_DIM ** -0.5 * LOG2_E

    def rope(t):
        return t * cos + pltpu.roll(t, HEAD_DIM // 2, 1) * sin

    for h in range(N_HEADS):
        sl = slice(h * HEAD_DIM, (h + 1) * HEAD_DIM)
        qo[:, sl] = (rope(q_ref[:, sl]) * scale).astype(BF16)
    ones = jnp.ones((q_ref.shape[0], HEAD_DIM), BF16)
    for g in range(N_KV):
        sl = slice(g * HEAD_DIM, (g + 1) * HEAD_DIM)
        kso[:, sl] = rope(ks_ref[:, sl]).astype(BF16)
        kwo[:, sl] = rope(kw_ref[:, sl]).astype(BF16)
        vso[:, 2 * g * HEAD_DIM:(2 * g + 1) * HEAD_DIM] = vs_ref[:, sl].astype(BF16)
        vso[:, (2 * g + 1) * HEAD_DIM:(2 * g + 2) * HEAD_DIM] = ones
        vwo[:, 2 * g * HEAD_DIM:(2 * g + 1) * HEAD_DIM] = vw_ref[:, sl].astype(BF16)
        vwo[:, (2 * g + 1) * HEAD_DIM:(2 * g + 2) * HEAD_DIM] = ones


def nsa_prep(proj, cos, sin_signed, *, rows=512, name):
    seq = proj.shape[0]
    kvb = lambda off: pl.BlockSpec((rows, KV_WIDTH), lambda i, o=off // KV_WIDTH: (i, o))
    return pl.pallas_call(
        _prep_body,
        grid=(seq // rows,),
        in_specs=[pl.BlockSpec((rows, ATTN_WIDTH), lambda i: (i, 0)),
                  kvb(KS_OFF), kvb(VS_OFF), kvb(KW_OFF), kvb(VW_OFF),
                  pl.BlockSpec((rows, HEAD_DIM), lambda i: (i, 0)),
                  pl.BlockSpec((rows, HEAD_DIM), lambda i: (i, 0))],
        out_specs=[pl.BlockSpec((rows, ATTN_WIDTH), lambda i: (i, 0)),
                   pl.BlockSpec((rows, KV_WIDTH), lambda i: (i, 0)),
                   pl.BlockSpec((rows, 2 * KV_WIDTH), lambda i: (i, 0)),
                   pl.BlockSpec((rows, KV_WIDTH), lambda i: (i, 0)),
                   pl.BlockSpec((rows, 2 * KV_WIDTH), lambda i: (i, 0))],
        out_shape=[jax.ShapeDtypeStruct((seq, ATTN_WIDTH), BF16),
                   jax.ShapeDtypeStruct((seq, KV_WIDTH), BF16),
                   jax.ShapeDtypeStruct((seq, 2 * KV_WIDTH), BF16),
                   jax.ShapeDtypeStruct((seq, KV_WIDTH), BF16),
                   jax.ShapeDtypeStruct((seq, 2 * KV_WIDTH), BF16)],
        compiler_params=pltpu.CompilerParams(dimension_semantics=("parallel",),
                                             vmem_limit_bytes=VMEM_LIMIT_BYTES),
        name=name,
    )(proj, proj, proj, proj, proj, cos, sin_signed)


def _compress_body(t_ref, cos_ref, sin_ref, pos_ref, w1_ref, w2_ref, o_ref, tr_ref, *, rope, aug):
    seq = t_ref.shape[0]
    ncp = seq // CMP_STRIDE
    t = t_ref[...]
    if rope:
        t = t * cos_ref[...] + pltpu.roll(t, HEAD_DIM // 2, 1) * sin_ref[...]
    tr_ref[...] = t
    acc_a = jnp.zeros((ncp, HEAD_DIM), F32)
    acc_b = jnp.zeros((ncp, HEAD_DIM), F32)
    for p in range(CMP_STRIDE):
        tp = tr_ref[pl.ds(p, ncp, stride=CMP_STRIDE), :]
        wa = w1_ref[p * HEAD_DIM:(p + 1) * HEAD_DIM, :].astype(BF16)
        wb = w1_ref[(CMP_STRIDE + p) * HEAD_DIM:(CMP_STRIDE + p + 1) * HEAD_DIM, :].astype(BF16)
        acc_a += jnp.dot((tp + pos_ref[p:p + 1, :]).astype(BF16), wa, preferred_element_type=F32)
        acc_b += jnp.dot((tp + pos_ref[CMP_STRIDE + p:CMP_STRIDE + p + 1, :]).astype(BF16), wb,
                         preferred_element_type=F32)
    pre = acc_a + pltpu.roll(acc_b, ncp - 1, 0)
    hid = pre * jax.nn.sigmoid(pre)
    out = jnp.dot(hid.astype(BF16), w2_ref[...].astype(BF16), preferred_element_type=F32)
    o_ref[0, :, :HEAD_DIM] = out.astype(BF16)
    if aug:
        o_ref[0, :, HEAD_DIM:] = jnp.ones((ncp, HEAD_DIM), BF16)


def nsa_compress(proj, col_off, cos, sin_signed, pos, w1, w2, *, rope, aug, name):
    seq = proj.shape[0]
    ncp = seq // CMP_STRIDE
    width = 2 * HEAD_DIM if aug else HEAD_DIM
    return pl.pallas_call(
        functools.partial(_compress_body, rope=rope, aug=aug),
        grid=(N_KV,),
        in_specs=[pl.BlockSpec((seq, HEAD_DIM), lambda g, o=col_off // HEAD_DIM: (0, o + g)),
                  pl.BlockSpec((seq, HEAD_DIM), lambda g: (0, 0)),
                  pl.BlockSpec((seq, HEAD_DIM), lambda g: (0, 0)),
                  pl.BlockSpec((CMP_LEN, HEAD_DIM), lambda g: (0, 0)),
                  pl.BlockSpec((CMP_LEN * HEAD_DIM, HEAD_DIM), lambda g: (0, 0)),
                  pl.BlockSpec((HEAD_DIM, HEAD_DIM), lambda g: (0, 0))],
        out_specs=pl.BlockSpec((1, ncp, width), lambda g: (g, 0, 0)),
        out_shape=jax.ShapeDtypeStruct((N_KV, ncp, width), BF16),
        scratch_shapes=[pltpu.VMEM((seq, HEAD_DIM), F32)],
        compiler_params=pltpu.CompilerParams(dimension_semantics=("parallel",),
                                             vmem_limit_bytes=VMEM_LIMIT_BYTES),
        name=name,
    )(proj, cos, sin_signed, pos, w1, w2)


def _nsa_body(q_ref, gate_ref, kc_ref, vc_ref, ks_ref, vs_ref, kw_ref, vw_ref, ovt_ref, o_ref,
              sc_ref, m_ref, acc_ref, ow_ref, *, seq):
    i = pl.program_id(1)
    t0 = i * Q_BLOCK
    ncp = seq // CMP_STRIDE
    nsel = seq // SEL_BLOCK
    top = min(SEL_TOP, nsel)
    nwin = WINDOW + Q_BLOCK
    rows = [slice(r * Q_BLOCK, (r + 1) * Q_BLOCK) for r in range(HEADS_PER_KV)]

    q4 = jnp.concatenate([q_ref[:, r * HEAD_DIM:(r + 1) * HEAD_DIM] for r in range(HEADS_PER_KV)], axis=0)

    s_c = _dot_nt(q4, kc_ref[0])
    c_end = lax.broadcasted_iota(jnp.int32, (Q_BLOCK, ncp), 1) * CMP_STRIDE + (CMP_LEN - 1)
    tq_c = t0 + lax.broadcasted_iota(jnp.int32, (Q_BLOCK, ncp), 0)
    cmask = c_end <= tq_c
    vc = vc_ref[0]
    psum = jnp.zeros((Q_BLOCK, ncp), F32)
    o_cmp = []
    for r in range(HEADS_PER_KV):
        s = jnp.where(cmask, s_c[rows[r]], NEG)
        m = jnp.max(s, axis=-1, keepdims=True)
        p = jnp.where(cmask, jnp.exp2(s - m), 0.0)
        pn = p / jnp.maximum(jnp.sum(p, axis=-1, keepdims=True), 1e-30)
        psum = psum + pn
        o_cmp.append(jnp.dot(pn.astype(BF16), vc[:, :HEAD_DIM], preferred_element_type=F32))

    w0 = pl.multiple_of(t0, Q_BLOCK)
    s_w = _dot_nt(q4, kw_ref[pl.ds(w0, nwin), :])
    vwin = vw_ref[pl.ds(w0, nwin), :]
    wpos = t0 - WINDOW + lax.broadcasted_iota(jnp.int32, (Q_BLOCK, nwin), 1)
    tq_w = t0 + lax.broadcasted_iota(jnp.int32, (Q_BLOCK, nwin), 0)
    wvalid = jnp.where(wpos <= tq_w, jnp.where(wpos > tq_w - WINDOW, jnp.where(wpos >= 0, 1.0, 0.0), 0.0), 0.0)
    wmask = wvalid > 0.5
    for r in range(HEADS_PER_KV):
        sm = jnp.where(wmask, s_w[rows[r]], NEG)
        pw = jnp.exp2(sm - jnp.max(sm, axis=-1, keepdims=True)).astype(BF16)
        res_w = jnp.dot(pw, vwin, preferred_element_type=F32)
        ow_ref[rows[r], :] = res_w[:, :HEAD_DIM] / res_w[:, HEAD_DIM:]

    ovt = ovt_ref[...]
    imp = sum(_dot_nt(ovt, part) for part in _split3(psum))
    blk = lax.broadcasted_iota(jnp.int32, (nsel, Q_BLOCK), 0)
    cur = (t0 + lax.broadcasted_iota(jnp.int32, (nsel, Q_BLOCK), 1)) // SEL_BLOCK
    blkf = blk.astype(F32)
    forced = jnp.where(blk == 0, 1.0, jnp.where(blk == cur, 1.0, jnp.where(blk == cur - 1, 1.0, 0.0)))
    imp = jnp.where(forced > 0.5, -jnp.inf, jnp.where(blk > cur, -BIG, imp))
    sel_t = forced
    for _ in range(top - 3):
        mx = jnp.max(imp, axis=0, keepdims=True)
        first = jnp.min(jnp.where(imp == mx, blkf, float(nsel)), axis=0, keepdims=True)
        hit = blkf == first
        sel_t = jnp.where(hit, 1.0, sel_t)
        imp = jnp.where(hit, -jnp.inf, imp)
    sel_t = jnp.where(blk <= cur, sel_t, 0.0)
    sel = sel_t.T.astype(BF16)

    n_chunks = (t0 + Q_BLOCK + SEL_CHUNK - 1) // SEL_CHUNK
    diff = (lax.broadcasted_iota(jnp.int32, (nsel, SEL_SUB), 0)
            - lax.broadcasted_iota(jnp.int32, (nsel, SEL_SUB), 1) // SEL_BLOCK)
    lane_k = lax.broadcasted_iota(jnp.int32, (Q_BLOCK, SEL_SUB), 1)
    tq_s = t0 + lax.broadcasted_iota(jnp.int32, (Q_BLOCK, SEL_SUB), 0)
    lane_tiles = SEL_CHUNK // HEAD_DIM
    m_ref[...] = jnp.full(m_ref.shape, NEG, F32)

    def pass1(c, carry):
        best = [m_ref[rows[r], :] for r in range(HEADS_PER_KV)]
        for u in range(SEL_CHUNK // SEL_SUB):
            k0 = pl.multiple_of(c * SEL_CHUNK + u * SEL_SUB, SEL_SUB)
            s = _dot_nt(q4, ks_ref[pl.ds(k0, SEL_SUB), :])
            expand = jnp.where(diff == k0 // SEL_BLOCK, 1.0, 0.0).astype(BF16)
            picked = jnp.dot(sel, expand, preferred_element_type=F32)
            picked = jnp.where(k0 + lane_k <= tq_s, picked, 0.0)
            mask = picked > 0.5
            for r in range(HEADS_PER_KV):
                sm = jnp.where(mask, s[rows[r]], NEG)
                sc_ref[c, rows[r], u * SEL_SUB:(u + 1) * SEL_SUB] = sm
                best[r] = jnp.maximum(best[r], jnp.maximum(sm[:, :HEAD_DIM], sm[:, HEAD_DIM:]))
        for r in range(HEADS_PER_KV):
            m_ref[rows[r], :] = best[r]
        return carry

    lax.fori_loop(0, n_chunks, pass1, 0)
    m_ref[...] = jnp.broadcast_to(jnp.max(m_ref[...], axis=-1, keepdims=True), m_ref.shape)

    acc_ref[...] = jnp.zeros(acc_ref.shape, F32)

    def pass2(c, carry):
        k0 = pl.multiple_of(c * SEL_CHUNK, SEL_CHUNK)
        p = jnp.exp2(sc_ref[c] - jnp.concatenate([m_ref[...]] * lane_tiles, axis=1)).astype(BF16)
        acc_ref[...] += jnp.dot(p, vs_ref[pl.ds(k0, SEL_CHUNK), :], preferred_element_type=F32)
        return carry

    lax.fori_loop(0, n_chunks, pass2, 0)

    gate = jax.nn.sigmoid(gate_ref[0])
    for r in range(HEADS_PER_KV):
        o_win = ow_ref[rows[r], :]
        res_s = acc_ref[rows[r], :]
        o_sel = res_s[:, :HEAD_DIM] / res_s[:, HEAD_DIM:]
        g_c = gate[:, r:r + 1]
        g_s = gate[:, HEADS_PER_KV + r:HEADS_PER_KV + r + 1]
        g_w = gate[:, 2 * HEADS_PER_KV + r:2 * HEADS_PER_KV + r + 1]
        o_ref[:, r * HEAD_DIM:(r + 1) * HEAD_DIM] = g_c * o_cmp[r] + g_s * o_sel + g_w * o_win


def _overlap_t(seq):
    ncp = seq // CMP_STRIDE
    nsel = seq // SEL_BLOCK
    c_start = np.arange(ncp) * CMP_STRIDE
    j_start = np.arange(nsel) * SEL_BLOCK
    ov = ((c_start[None, :] < j_start[:, None] + SEL_BLOCK) & (c_start[None, :] + CMP_LEN > j_start[:, None])
          & (np.arange(ncp)[None, :] < ncp - 1))
    return jnp.asarray(ov.astype(np.float32), BF16)


def nsa_attention(q_r, gates_g, kc, vc_aug, ks_r, vs_aug, kw_pad, vw_pad, *, name):
    seq = q_r.shape[0]
    nq = seq // Q_BLOCK
    ncp = seq // CMP_STRIDE
    grp = HEADS_PER_KV * HEAD_DIM
    return pl.pallas_call(
        functools.partial(_nsa_body, seq=seq),
        grid=(N_KV, nq),
        in_specs=[pl.BlockSpec((Q_BLOCK, grp), lambda g, i: (i, g)),
                  pl.BlockSpec((1, Q_BLOCK, HEAD_DIM), lambda g, i: (g, i, 0)),
                  pl.BlockSpec((1, ncp, HEAD_DIM), lambda g, i: (g, 0, 0)),
                  pl.BlockSpec((1, ncp, 2 * HEAD_DIM), lambda g, i: (g, 0, 0)),
                  pl.BlockSpec((seq, HEAD_DIM), lambda g, i: (0, g)),
                  pl.BlockSpec((seq, 2 * HEAD_DIM), lambda g, i: (0, g)),
                  pl.BlockSpec((seq + WINDOW, HEAD_DIM), lambda g, i: (0, g)),
                  pl.BlockSpec((seq + WINDOW, 2 * HEAD_DIM), lambda g, i: (0, g)),
                  pl.BlockSpec((seq // SEL_BLOCK, ncp), lambda g, i: (0, 0))],
        out_specs=pl.BlockSpec((Q_BLOCK, grp), lambda g, i: (i, g)),
        out_shape=jax.ShapeDtypeStruct((seq, ATTN_WIDTH), F32),
        scratch_shapes=[pltpu.VMEM((seq // SEL_CHUNK, HEADS_PER_KV * Q_BLOCK, SEL_CHUNK), F32),
                        pltpu.VMEM((HEADS_PER_KV * Q_BLOCK, HEAD_DIM), F32),
                        pltpu.VMEM((HEADS_PER_KV * Q_BLOCK, 2 * HEAD_DIM), F32),
                        pltpu.VMEM((HEADS_PER_KV * Q_BLOCK, HEAD_DIM), F32)],
        compiler_params=pltpu.CompilerParams(dimension_semantics=("parallel", "arbitrary"),
                                             vmem_limit_bytes=VMEM_LIMIT_BYTES),
        name=name,
    )(q_r, gates_g, kc, vc_aug, ks_r, vs_aug, kw_pad, vw_pad, _overlap_t(seq))


def nsa_mixer(proj, gate_logits, cmp_pos, cmp_w1, cmp_w2, cos, sin_signed, *, tag):
    seq = proj.shape[0]
    q_r, ks_r, vs_aug, kw_r, vw_aug = nsa_prep(proj, cos, sin_signed, name="nsa_prep" + tag)
    kc = nsa_compress(proj, KC_OFF, cos, sin_signed, cmp_pos[0], cmp_w1[0], cmp_w2[0],
                      rope=True, aug=False, name="nsa_cmp_k" + tag)
    vc_aug = nsa_compress(proj, VC_OFF, cos, sin_signed, cmp_pos[1], cmp_w1[1], cmp_w2[1],
                          rope=False, aug=True, name="nsa_cmp_v" + tag)
    kw_pad = jnp.pad(kw_r, ((WINDOW, 0), (0, 0)))
    vw_pad = jnp.pad(vw_aug, ((WINDOW, 0), (0, 0)))
    gl = gate_logits.reshape(seq, 3, N_KV, HEADS_PER_KV).transpose(2, 0, 1, 3).reshape(N_KV, seq, 3 * HEADS_PER_KV)
    gl = jnp.pad(gl, ((0, 0), (0, 0), (0, HEAD_DIM - 3 * HEADS_PER_KV)))
    return nsa_attention(q_r, gl, kc, vc_aug, ks_r, vs_aug, kw_pad, vw_pad, name="nsa_attention" + tag)


LANES = 128
SUBLANES = 8


def _exact_dot(x, m):
    return sum(jnp.dot(part, m, preferred_element_type=F32) for part in _split3(x))


def _exact_dot_left(m, x):
    return sum(jnp.dot(m, part, preferred_element_type=F32) for part in _split3(x))


def _ssd_body(z_ref, xbc_ref, dt_ref, convw_ref, convb_ref, dtb_ref, alog_ref, dskip_ref, normw_ref,
              tril_ref, hexp_ref, o_ref, xcat_ref, act_ref, acsx_ref, dtx_ref, y_ref, state_ref):
    c = pl.program_id(0)
    L = SSD_CHUNK
    half = lax.broadcasted_iota(jnp.int32, (L, LANES), 1) < SSM_HEAD_DIM

    @pl.when(c == 0)
    def _():
        xcat_ref[0:SUBLANES, :] = jnp.zeros((SUBLANES, CONV_DIM), F32)
        state_ref[...] = jnp.zeros(state_ref.shape, F32)

    xcat_ref[SUBLANES:SUBLANES + L, :] = xbc_ref[...]
    slab = 512
    for j in range(CONV_DIM // slab):
        cs = slice(j * slab, (j + 1) * slab)
        conv = convb_ref[:, cs] + xcat_ref[pl.ds(SUBLANES - 3, L), cs] * convw_ref[0:1, cs]
        for k in range(1, CONV_WIDTH):
            conv = conv + xcat_ref[pl.ds(SUBLANES - 3 + k, L), cs] * convw_ref[k:k + 1, cs]
        act_ref[:, cs] = conv * jax.nn.sigmoid(conv)
    xcat_ref[0:SUBLANES, :] = xcat_ref[L:L + SUBLANES, :]

    raw = dt_ref[...] + dtb_ref[...]
    dtv = jnp.maximum(raw, 0.0) + jnp.log1p(jnp.exp(-jnp.abs(raw)))
    a = dtv * (-jnp.exp(alog_ref[...]))
    acs = _exact_dot_left(tril_ref[...], a)
    acs_t = acs.T
    hexp = hexp_ref[...]
    acsx_ref[...] = _exact_dot(acs, hexp)
    dtx_ref[...] = _exact_dot(dtv, hexp)

    tri = lax.broadcasted_iota(jnp.int32, (L, L), 0) >= lax.broadcasted_iota(jnp.int32, (L, L), 1)
    bcol = SSM_WIDTH
    ccol = SSM_WIDTH + SSM_GROUPS * SSM_STATE
    for g in range(SSM_GROUPS):
        bg = act_ref[:, bcol + g * SSM_STATE:bcol + (g + 1) * SSM_STATE]
        cg = act_ref[:, ccol + g * SSM_STATE:ccol + (g + 1) * SSM_STATE].astype(BF16)
        cb = _dot_nt(cg, bg.astype(BF16))
        bg_t = bg.T.astype(BF16)
        for jj in range(2):
            j = 2 * g + jj
            h0, h1 = 2 * j, 2 * j + 1
            ts = slice(j * LANES, (j + 1) * LANES)
            xs = act_ref[:, ts]
            col0 = acsx_ref[:, h0 * LANES:(h0 + 1) * LANES]
            col1 = acsx_ref[:, h1 * LANES:(h1 + 1) * LANES]
            acs_pair = jnp.where(half, col0, col1)
            dt_pair = jnp.where(half, dtx_ref[:, h0 * LANES:(h0 + 1) * LANES], dtx_ref[:, h1 * LANES:(h1 + 1) * LANES])
            xdt = xs * dt_pair
            xdt16 = xdt.astype(BF16)
            zero16 = jnp.zeros_like(xdt16)
            seg0 = jnp.concatenate([col0, col0], axis=1) - acs_t[h0:h0 + 1, :]
            seg1 = jnp.concatenate([col1, col1], axis=1) - acs_t[h1:h1 + 1, :]
            g0 = (cb * jnp.exp(jnp.where(tri, seg0, NEG))).astype(BF16)
            g1 = (cb * jnp.exp(jnp.where(tri, seg1, NEG))).astype(BF16)
            y = (jnp.dot(g0, jnp.where(half, xdt16, zero16), preferred_element_type=F32)
                 + jnp.dot(g1, jnp.where(half, zero16, xdt16), preferred_element_type=F32))
            st = state_ref[j]
            y = y + jnp.dot(cg, st.astype(BF16), preferred_element_type=F32) * jnp.exp(acs_pair)
            last = acs_pair[L - 1:L, :]
            xw = (xdt * jnp.exp(last - acs_pair)).astype(BF16)
            state_ref[j] = jnp.exp(last) * st + jnp.dot(bg_t, xw, preferred_element_type=F32)
            y_ref[:, ts] = y + dskip_ref[:, ts] * xs

    gw = SSM_WIDTH // SSM_GROUPS
    for g in range(SSM_GROUPS):
        gs = slice(g * gw, (g + 1) * gw)
        zz = z_ref[:, gs]
        yz = y_ref[:, gs] * (zz * jax.nn.sigmoid(zz))
        ms = jnp.mean(yz * yz, axis=-1, keepdims=True)
        o_ref[:, gs] = (yz * lax.rsqrt(ms + EPS) * normw_ref[:, gs]).astype(o_ref.dtype)


def ssd_mixer(proj, z_col, xbc_col, small, conv_w, conv_b, dt_bias, a_log, d_skip, norm_w, *, name):
    seq = proj.shape[0]
    L = SSD_CHUNK
    pad = lambda v: jnp.pad(v.astype(F32), (0, LANES - v.shape[0])).reshape(1, LANES)
    tril = jnp.asarray(np.tril(np.ones((L, L), np.float32)), BF16)
    hexp = jnp.asarray((np.arange(LANES)[:, None] == np.arange(SSM_HEADS * LANES)[None, :] // LANES)
                       .astype(np.float32), BF16)
    dskip_x = jnp.repeat(d_skip.astype(F32), SSM_HEAD_DIM).reshape(1, SSM_WIDTH)
    full = lambda shape: pl.BlockSpec(shape, lambda c: (0,) * len(shape))
    return pl.pallas_call(
        _ssd_body,
        grid=(seq // L,),
        in_specs=[pl.BlockSpec((L, SSM_WIDTH), lambda c, o=z_col // SSM_WIDTH: (c, o)),
                  pl.BlockSpec((L, CONV_DIM), lambda c, o=xbc_col // CONV_DIM: (c, o)),
                  pl.BlockSpec((L, LANES), lambda c: (c, 0)),
                  full((CONV_WIDTH, CONV_DIM)), full((1, CONV_DIM)), full((1, LANES)), full((1, LANES)),
                  full((1, SSM_WIDTH)), full((1, SSM_WIDTH)), full((L, L)), full((LANES, SSM_HEADS * LANES))],
        out_specs=pl.BlockSpec((L, SSM_WIDTH), lambda c: (c, 0)),
        out_shape=jax.ShapeDtypeStruct((seq, SSM_WIDTH), BF16),
        scratch_shapes=[pltpu.VMEM((L + SUBLANES, CONV_DIM), F32),
                        pltpu.VMEM((L, CONV_DIM), F32),
                        pltpu.VMEM((L, SSM_HEADS * LANES), F32),
                        pltpu.VMEM((L, SSM_HEADS * LANES), F32),
                        pltpu.VMEM((L, SSM_WIDTH), F32),
                        pltpu.VMEM((SSM_HEADS // 2, SSM_STATE, LANES), F32)],
        compiler_params=pltpu.CompilerParams(dimension_semantics=("arbitrary",),
                                             vmem_limit_bytes=VMEM_LIMIT_BYTES),
        name=name,
    )(proj, proj, small, conv_w, conv_b.reshape(1, CONV_DIM), pad(dt_bias), pad(a_log), dskip_x,
      norm_w.reshape(1, SSM_WIDTH).astype(F32), tril, hexp)


def _rope_tables(seq):
    inv = 1.0 / (ROPE_THETA ** (jnp.arange(0, HEAD_DIM, 2, dtype=F32) / HEAD_DIM))
    ang = jnp.arange(seq, dtype=F32)[:, None] * inv[None, :]
    ang = jnp.concatenate([ang, ang], axis=-1)
    return jnp.cos(ang), jnp.sin(ang)


def kernel(x, norm_mix, w_in, cmp_pos, cmp_w1, cmp_w2, attn_norm, conv_w, conv_b, dt_bias, a_log,
           d_skip, ssm_norm, w_out, norm_mlp, w_up, w_down, norm_final):
    batch, seq, _ = x.shape
    assert batch == 1
    cos, sin = _rope_tables(seq)
    half_sign = jnp.concatenate([-jnp.ones((HEAD_DIM // 2,), F32), jnp.ones((HEAD_DIM // 2,), F32)])
    sin_signed = sin * half_sign
    h = x.reshape(seq, D_MODEL)
    w_big = jnp.concatenate([w_in[:, :, :ATTN_WIDTH], w_in[:, :, W_Z_OFF:W_DT_OFF],
                             w_in[:, :, W_KV_OFF:W_GATE_OFF]], axis=2).astype(BF16)
    w_small = jnp.concatenate(
        [w_in[:, :, W_DT_OFF:], w_in[:, :, W_GATE_OFF:W_Z_OFF],
         jnp.zeros((DEPTH, D_MODEL, SMALL_COLS - SSM_HEADS - GATE_COLS), F32)], axis=2).astype(BF16)
    for l in range(DEPTH):
        xn = rmsnorm(h, norm_mix[l], out_dtype=BF16, name=f"norm_mix{l}")
        proj = matmul(xn, w_big, l, name=f"in_proj{l}")
        small = matmul(xn, w_small, l, name=f"in_proj_small{l}")
        attn = nsa_mixer(proj, small[:, SSM_HEADS:SSM_HEADS + GATE_COLS], cmp_pos[l], cmp_w1[l], cmp_w2[l],
                         cos, sin_signed, tag=str(l))
        ssm = ssd_mixer(proj, Z_COL, XBC_COL, small, conv_w[l], conv_b[l], dt_bias[l], a_log[l], d_skip[l],
                        ssm_norm[l], name=f"ssd{l}")
        mix = norm_concat(attn, attn_norm[l], ssm, name=f"norm_concat{l}")
        h = matmul(mix, w_out, l, residual=h, bn=512, name=f"out_proj{l}")
        hn = rmsnorm(h, norm_mlp[l], out_dtype=BF16, name=f"norm_mlp{l}")
        u = matmul(hn, w_up, l, act="relu2", out_dtype=BF16, bn=512, name=f"mlp_up{l}")
        h = matmul(u, w_down, l, residual=h, bk=4096, bn=512, name=f"mlp_down{l}")
    out = rmsnorm(h, norm_final, out_dtype=F32, name="norm_final")
    return out.reshape(batch, seq, D_MODEL)
```

```python
import functools

import jax
import jax.numpy as jnp
import numpy as np
from jax import lax
from jax.experimental import pallas as pl
from jax.experimental.pallas import tpu as pltpu

F32 = jnp.float32
BF16 = jnp.bfloat16

D_MODEL = 4096
DEPTH = 2
ATTN_WIDTH = 2048
SSM_WIDTH = 2048
HEAD_DIM = 128
N_HEADS = 16
N_KV = 4
HEADS_PER_KV = 4
KV_WIDTH = 512
ROPE_THETA = 10000.0
CMP_LEN = 32
CMP_STRIDE = 16
SEL_BLOCK = 64
SEL_TOP = 16
WINDOW = 512
Q_BLOCK = 128
BIG = 1e9
SSM_HEAD_DIM = 64
SSM_HEADS = 32
SSM_GROUPS = 8
SSM_STATE = 128
CONV_WIDTH = 4
SSD_CHUNK = 256
CONV_DIM = SSM_WIDTH + 2 * SSM_GROUPS * SSM_STATE
D_FF = 4 * D_MODEL
EPS = 1e-5
NEG = -1e30

VMEM_LIMIT_BYTES = 52 * 1024 * 1024

GATE_COLS = 3 * N_HEADS
W_KV_OFF = ATTN_WIDTH
W_GATE_OFF = W_KV_OFF + 6 * KV_WIDTH
W_Z_OFF = W_GATE_OFF + GATE_COLS
W_XBC_OFF = W_Z_OFF + SSM_WIDTH
W_DT_OFF = W_XBC_OFF + CONV_DIM
ATTN_PROJ_COLS = W_GATE_OFF
KC_OFF, VC_OFF, KS_OFF, VS_OFF, KW_OFF, VW_OFF = (W_KV_OFF + i * KV_WIDTH for i in range(6))
XBC_COL = 0
Z_COL = CONV_DIM
SMALL_COLS = 128


def _matmul_body(*refs, nk, act, has_res):
    a_ref, b_ref = refs[0], refs[1]
    r_ref = refs[2] if has_res else None
    o_ref = refs[3] if has_res else refs[2]
    part = jnp.dot(a_ref[...], b_ref[...].astype(BF16), preferred_element_type=F32)

    def finish(acc):
        if act == "relu2":
            acc = jnp.square(jnp.maximum(acc, 0.0))
        if has_res:
            acc = acc + r_ref[...]
        o_ref[...] = acc.astype(o_ref.dtype)

    if nk == 1:
        finish(part)
    else:
        acc_ref = refs[-1]
        k = pl.program_id(2)

        @pl.when(k == 0)
        def _():
            acc_ref[...] = part

        @pl.when(k > 0)
        def _():
            acc_ref[...] += part

        @pl.when(k == nk - 1)
        def _():
            finish(acc_ref[...])


def matmul(a, b, layer, *, residual=None, act=None, out_dtype=F32, bm=1024, bn=1024, bk=None, name):
    m, kdim = a.shape
    _, _, n = b.shape
    bk = kdim if bk is None else bk
    bn = min(bn, n)
    assert m % bm == 0 and n % bn == 0 and kdim % bk == 0
    nk = kdim // bk
    in_specs = [pl.BlockSpec((bm, bk), lambda i, j, k: (i, k)),
                pl.BlockSpec((None, bk, bn), lambda i, j, k: (layer, k, j))]
    args = [a, b]
    if residual is not None:
        in_specs.append(pl.BlockSpec((bm, bn), lambda i, j, k: (i, j)))
        args.append(residual)
    return pl.pallas_call(
        functools.partial(_matmul_body, nk=nk, act=act, has_res=residual is not None),
        grid=(m // bm, n // bn, nk),
        in_specs=in_specs,
        out_specs=pl.BlockSpec((bm, bn), lambda i, j, k: (i, j)),
        out_shape=jax.ShapeDtypeStruct((m, n), out_dtype),
        scratch_shapes=[pltpu.VMEM((bm, bn), F32)] if nk > 1 else [],
        compiler_params=pltpu.CompilerParams(
            dimension_semantics=("parallel", "parallel", "arbitrary"),
            vmem_limit_bytes=VMEM_LIMIT_BYTES),
        name=name,
    )(*args)


def _matmul_ws_body(*refs, act, has_res):
    a_ref, w_ref = refs[0], refs[1]
    r_ref = refs[2] if has_res else None
    o_ref = refs[3] if has_res else refs[2]
    w16_ref = refs[-1]

    @pl.when(pl.program_id(1) == 0)
    def _():
        w16_ref[...] = w_ref[...].astype(BF16)

    acc = jnp.dot(a_ref[...], w16_ref[...], preferred_element_type=F32)
    if act == "relu2":
        acc = jnp.square(jnp.maximum(acc, 0.0))
    if has_res:
        acc = acc + r_ref[...]
    o_ref[...] = acc.astype(o_ref.dtype)


def matmul_ws(a, w, layer, n_cols, *, residual=None, act=None, out_dtype=F32, bm=1024, bn=512, name):
    m, kdim = a.shape
    assert m % bm == 0 and n_cols % bn == 0 and w.shape[1] == kdim and n_cols <= w.shape[2]
    in_specs = [pl.BlockSpec((bm, kdim), lambda j, i: (i, 0)),
                pl.BlockSpec((None, kdim, bn), lambda j, i: (layer, 0, j))]
    args = [a, w]
    if residual is not None:
        in_specs.append(pl.BlockSpec((bm, bn), lambda j, i: (i, j)))
        args.append(residual)
    return pl.pallas_call(
        functools.partial(_matmul_ws_body, act=act, has_res=residual is not None),
        grid=(n_cols // bn, m // bm),
        in_specs=in_specs,
        out_specs=pl.BlockSpec((bm, bn), lambda j, i: (i, j)),
        out_shape=jax.ShapeDtypeStruct((m, n_cols), out_dtype),
        scratch_shapes=[pltpu.VMEM((kdim, bn), BF16)],
        compiler_params=pltpu.CompilerParams(
            dimension_semantics=("arbitrary", "arbitrary"),
            vmem_limit_bytes=VMEM_LIMIT_BYTES),
        name=name,
    )(*args)


def _rmsnorm_body(x_ref, w_ref, o_ref):
    x = x_ref[...]
    y = x * lax.rsqrt(jnp.mean(x * x, axis=-1, keepdims=True) + EPS)
    o_ref[...] = (y * w_ref[...]).astype(o_ref.dtype)


def rmsnorm(x, w, *, out_dtype, rows=256, name):
    m, d = x.shape
    return pl.pallas_call(
        _rmsnorm_body,
        grid=(m // rows,),
        in_specs=[pl.BlockSpec((rows, d), lambda i: (i, 0)),
                  pl.BlockSpec((1, d), lambda i: (0, 0))],
        out_specs=pl.BlockSpec((rows, d), lambda i: (i, 0)),
        out_shape=jax.ShapeDtypeStruct((m, d), out_dtype),
        compiler_params=pltpu.CompilerParams(dimension_semantics=("parallel",)),
        name=name,
    )(x, w.reshape(1, d))


def _norm_concat_body(a_ref, w_ref, s_ref, o_ref):
    x = a_ref[...]
    y = x * lax.rsqrt(jnp.mean(x * x, axis=-1, keepdims=True) + EPS)
    o_ref[:, :ATTN_WIDTH] = (y * w_ref[...]).astype(o_ref.dtype)
    o_ref[:, ATTN_WIDTH:] = s_ref[...]


def norm_concat(attn, w, ssm, *, rows=256, name):
    m = attn.shape[0]
    return pl.pallas_call(
        _norm_concat_body,
        grid=(m // rows,),
        in_specs=[pl.BlockSpec((rows, ATTN_WIDTH), lambda i: (i, 0)),
                  pl.BlockSpec((1, ATTN_WIDTH), lambda i: (0, 0)),
                  pl.BlockSpec((rows, SSM_WIDTH), lambda i: (i, 0))],
        out_specs=pl.BlockSpec((rows, D_MODEL), lambda i: (i, 0)),
        out_shape=jax.ShapeDtypeStruct((m, D_MODEL), BF16),
        compiler_params=pltpu.CompilerParams(dimension_semantics=("parallel",)),
        name=name,
    )(attn, w.reshape(1, ATTN_WIDTH), ssm)


SEL_CHUNK = 1024
SEL_SUB = 256
LOG2_E = 1.4426950408889634


def _split3(x):
    hi = x.astype(BF16)
    r1 = x - hi.astype(F32)
    mid = r1.astype(BF16)
    lo = (r1 - mid.astype(F32)).astype(BF16)
    return hi, mid, lo


def _dot_nt(a, b):
    return lax.dot_general(a, b, (((1,), (1,)), ((), ())), preferred_element_type=F32)


def _prep_body(q_ref, ks_ref, vs_ref, kw_ref, vw_ref, cos_ref, sin_ref, qo, kso, vso, kwo, vwo):
    cos = cos_ref[...]
    sin = sin_ref[...]
    scale = HEAD_DIM ** -0.5 * LOG2_E

    def rope(t):
        return t * cos + pltpu.roll(t, HEAD_DIM // 2, 1) * sin

    for h in range(N_HEADS):
        sl = slice(h * HEAD_DIM, (h + 1) * HEAD_DIM)
        qo[:, sl] = (rope(q_ref[:, sl]) * scale).astype(BF16)
    ones = jnp.ones((q_ref.shape[0], HEAD_DIM), BF16)
    for g in range(N_KV):
        sl = slice(g * HEAD_DIM, (g + 1) * HEAD_DIM)
        kso[:, sl] = rope(ks_ref[:, sl]).astype(BF16)
        kwo[:, sl] = rope(kw_ref[:, sl]).astype(BF16)
        vso[:, 2 * g * HEAD_DIM:(2 * g + 1) * HEAD_DIM] = vs_ref[:, sl].astype(BF16)
        vso[:, (2 * g + 1) * HEAD_DIM:(2 * g + 2) * HEAD_DIM] = ones
        vwo[:, 2 * g * HEAD_DIM:(2 * g + 1) * HEAD_DIM] = vw_ref[:, sl].astype(BF16)
        vwo[:, (2 * g + 1) * HEAD_DIM:(2 * g + 2) * HEAD_DIM] = ones


def nsa_prep(proj, cos, sin_signed, *, rows=512, name):
    seq = proj.shape[0]
    kvb = lambda off: pl.BlockSpec((rows, KV_WIDTH), lambda i, o=off // KV_WIDTH: (i, o))
    return pl.pallas_call(
        _prep_body,
        grid=(seq // rows,),
        in_specs=[pl.BlockSpec((rows, ATTN_WIDTH), lambda i: (i, 0)),
                  kvb(KS_OFF), kvb(VS_OFF), kvb(KW_OFF), kvb(VW_OFF),
                  pl.BlockSpec((rows, HEAD_DIM), lambda i: (i, 0)),
                  pl.BlockSpec((rows, HEAD_DIM), lambda i: (i, 0))],
        out_specs=[pl.BlockSpec((rows, ATTN_WIDTH), lambda i: (i, 0)),
                   pl.BlockSpec((rows, KV_WIDTH), lambda i: (i, 0)),
                   pl.BlockSpec((rows, 2 * KV_WIDTH), lambda i: (i, 0)),
                   pl.BlockSpec((rows, KV_WIDTH), lambda i: (i, 0)),
                   pl.BlockSpec((rows, 2 * KV_WIDTH), lambda i: (i, 0))],
        out_shape=[jax.ShapeDtypeStruct((seq, ATTN_WIDTH), BF16),
                   jax.ShapeDtypeStruct((seq, KV_WIDTH), BF16),
                   jax.ShapeDtypeStruct((seq, 2 * KV_WIDTH), BF16),
                   jax.ShapeDtypeStruct((seq, KV_WIDTH), BF16),
                   jax.ShapeDtypeStruct((seq, 2 * KV_WIDTH), BF16)],
        compiler_params=pltpu.CompilerParams(dimension_semantics=("parallel",),
                                             vmem_limit_bytes=VMEM_LIMIT_BYTES),
        name=name,
    )(proj, proj, proj, proj, proj, cos, sin_signed)


def _compress_body(t_ref, cos_ref, sin_ref, pos_ref, w1_ref, w2_ref, o_ref, tr_ref, *, rope, aug):
    seq = t_ref.shape[0]
    ncp = seq // CMP_STRIDE
    t = t_ref[...]
    if rope:
        t = t * cos_ref[...] + pltpu.roll(t, HEAD_DIM // 2, 1) * sin_ref[...]
    tr_ref[...] = t
    acc_a = jnp.zeros((ncp, HEAD_DIM), F32)
    acc_b = jnp.zeros((ncp, HEAD_DIM), F32)
    for p in range(CMP_STRIDE):
        tp = tr_ref[pl.ds(p, ncp, stride=CMP_STRIDE), :]
        wa = w1_ref[p * HEAD_DIM:(p + 1) * HEAD_DIM, :].astype(BF16)
        wb = w1_ref[(CMP_STRIDE + p) * HEAD_DIM:(CMP_STRIDE + p + 1) * HEAD_DIM, :].astype(BF16)
        acc_a += jnp.dot((tp + pos_ref[p:p + 1, :]).astype(BF16), wa, preferred_element_type=F32)
        acc_b += jnp.dot((tp + pos_ref[CMP_STRIDE + p:CMP_STRIDE + p + 1, :]).astype(BF16), wb,
                         preferred_element_type=F32)
    pre = acc_a + pltpu.roll(acc_b, ncp - 1, 0)
    hid = pre * jax.nn.sigmoid(pre)
    out = jnp.dot(hid.astype(BF16), w2_ref[...].astype(BF16), preferred_element_type=F32)
    o_ref[0, :, :HEAD_DIM] = out.astype(BF16)
    if aug:
        o_ref[0, :, HEAD_DIM:] = jnp.ones((ncp, HEAD_DIM), BF16)


def nsa_compress(proj, col_off, cos, sin_signed, pos, w1, w2, *, rope, aug, name):
    seq = proj.shape[0]
    ncp = seq // CMP_STRIDE
    width = 2 * HEAD_DIM if aug else HEAD_DIM
    return pl.pallas_call(
        functools.partial(_compress_body, rope=rope, aug=aug),
        grid=(N_KV,),
        in_specs=[pl.BlockSpec((seq, HEAD_DIM), lambda g, o=col_off // HEAD_DIM: (0, o + g)),
                  pl.BlockSpec((seq, HEAD_DIM), lambda g: (0, 0)),
                  pl.BlockSpec((seq, HEAD_DIM), lambda g: (0, 0)),
                  pl.BlockSpec((CMP_LEN, HEAD_DIM), lambda g: (0, 0)),
                  pl.BlockSpec((CMP_LEN * HEAD_DIM, HEAD_DIM), lambda g: (0, 0)),
                  pl.BlockSpec((HEAD_DIM, HEAD_DIM), lambda g: (0, 0))],
        out_specs=pl.BlockSpec((1, ncp, width), lambda g: (g, 0, 0)),
        out_shape=jax.ShapeDtypeStruct((N_KV, ncp, width), BF16),
        scratch_shapes=[pltpu.VMEM((seq, HEAD_DIM), F32)],
        compiler_params=pltpu.CompilerParams(dimension_semantics=("parallel",),
                                             vmem_limit_bytes=VMEM_LIMIT_BYTES),
        name=name,
    )(proj, cos, sin_signed, pos, w1, w2)


def _nsa_body(q_ref, gate_ref, kc_ref, vc_ref, ks_ref, vs_ref, kw_ref, vw_ref, ovt_ref, o_ref,
              sc_ref, m_ref, acc_ref, ow_ref, *, seq):
    i = pl.program_id(1)
    t0 = i * Q_BLOCK
    ncp = seq // CMP_STRIDE
    nsel = seq // SEL_BLOCK
    top = min(SEL_TOP, nsel)
    nwin = WINDOW + Q_BLOCK
    rows = [slice(r * Q_BLOCK, (r + 1) * Q_BLOCK) for r in range(HEADS_PER_KV)]

    q4 = jnp.concatenate([q_ref[:, r * HEAD_DIM:(r + 1) * HEAD_DIM] for r in range(HEADS_PER_KV)], axis=0)

    s_c = _dot_nt(q4, kc_ref[0])
    c_end = lax.broadcasted_iota(jnp.int32, (Q_BLOCK, ncp), 1) * CMP_STRIDE + (CMP_LEN - 1)
    tq_c = t0 + lax.broadcasted_iota(jnp.int32, (Q_BLOCK, ncp), 0)
    cmask = c_end <= tq_c
    vc = vc_ref[0]
    psum = jnp.zeros((Q_BLOCK, ncp), F32)
    o_cmp = []
    for r in range(HEADS_PER_KV):
        s = jnp.where(cmask, s_c[rows[r]], NEG)
        m = jnp.max(s, axis=-1, keepdims=True)
        p = jnp.where(cmask, jnp.exp2(s - m), 0.0)
        pn = p / jnp.maximum(jnp.sum(p, axis=-1, keepdims=True), 1e-30)
        psum = psum + pn
        o_cmp.append(jnp.dot(pn.astype(BF16), vc[:, :HEAD_DIM], preferred_element_type=F32))

    ovt = ovt_ref[...]
    imp = sum(_dot_nt(ovt, part) for part in _split3(psum))
    blk = lax.broadcasted_iota(jnp.int32, (nsel, Q_BLOCK), 0)
    cur = (t0 + lax.broadcasted_iota(jnp.int32, (nsel, Q_BLOCK), 1)) // SEL_BLOCK
    blkf = blk.astype(F32)
    forced = jnp.where(blk == 0, 1.0, jnp.where(blk == cur, 1.0, jnp.where(blk == cur - 1, 1.0, 0.0)))
    imp = jnp.where(forced > 0.5, -jnp.inf, jnp.where(blk > cur, -BIG, imp))
    sel_t = forced

    def topk_round(imp, sel_t):
        mx = jnp.max(imp, axis=0, keepdims=True)
        first = jnp.min(jnp.where(imp == mx, blkf, float(nsel)), axis=0, keepdims=True)
        hit = blkf == first
        return jnp.where(hit, -jnp.inf, imp), jnp.where(hit, 1.0, sel_t)

    w0 = pl.multiple_of(t0, Q_BLOCK)
    s_w = _dot_nt(q4, kw_ref[pl.ds(w0, nwin), :])
    vwin = vw_ref[pl.ds(w0, nwin), :]
    wpos = t0 - WINDOW + lax.broadcasted_iota(jnp.int32, (Q_BLOCK, nwin), 1)
    tq_w = t0 + lax.broadcasted_iota(jnp.int32, (Q_BLOCK, nwin), 0)
    wvalid = jnp.where(wpos <= tq_w, jnp.where(wpos > tq_w - WINDOW, jnp.where(wpos >= 0, 1.0, 0.0), 0.0), 0.0)
    wmask = wvalid > 0.5
    rounds = top - 3
    per_head = -(-rounds // HEADS_PER_KV)
    for r in range(HEADS_PER_KV):
        sm = jnp.where(wmask, s_w[rows[r]], NEG)
        pw = jnp.exp2(sm - jnp.max(sm, axis=-1, keepdims=True)).astype(BF16)
        res_w = jnp.dot(pw, vwin, preferred_element_type=F32)
        ow_ref[rows[r], :] = res_w[:, :HEAD_DIM] / res_w[:, HEAD_DIM:]
        for _ in range(min(per_head, rounds - r * per_head)):
            imp, sel_t = topk_round(imp, sel_t)
    sel_t = jnp.where(blk <= cur, sel_t, 0.0)
    sel = sel_t.T.astype(BF16)

    n_chunks = (t0 + Q_BLOCK + SEL_CHUNK - 1) // SEL_CHUNK
    diff = (lax.broadcasted_iota(jnp.int32, (nsel, SEL_SUB), 0)
            - lax.broadcasted_iota(jnp.int32, (nsel, SEL_SUB), 1) // SEL_BLOCK)
    lane_k = lax.broadcasted_iota(jnp.int32, (Q_BLOCK, SEL_SUB), 1)
    tq_s = t0 + lax.broadcasted_iota(jnp.int32, (Q_BLOCK, SEL_SUB), 0)
    lane_tiles = SEL_CHUNK // HEAD_DIM
    m_ref[...] = jnp.full(m_ref.shape, NEG, F32)

    def pass1(c, carry):
        best = [m_ref[rows[r], :] for r in range(HEADS_PER_KV)]
        for u in range(SEL_CHUNK // SEL_SUB):
            k0 = pl.multiple_of(c * SEL_CHUNK + u * SEL_SUB, SEL_SUB)
            s = _dot_nt(q4, ks_ref[pl.ds(k0, SEL_SUB), :])
            expand = jnp.where(diff == k0 // SEL_BLOCK, 1.0, 0.0).astype(BF16)
            picked = jnp.dot(sel, expand, preferred_element_type=F32)
            picked = jnp.where(k0 + lane_k <= tq_s, picked, 0.0)
            mask = picked > 0.5
            for r in range(HEADS_PER_KV):
                sm = jnp.where(mask, s[rows[r]], NEG)
                sc_ref[c, rows[r], u * SEL_SUB:(u + 1) * SEL_SUB] = sm
                best[r] = jnp.maximum(best[r], jnp.maximum(sm[:, :HEAD_DIM], sm[:, HEAD_DIM:]))
        for r in range(HEADS_PER_KV):
            m_ref[rows[r], :] = best[r]
        return carry

    lax.fori_loop(0, n_chunks, pass1, 0)
    m_ref[...] = jnp.broadcast_to(jnp.max(m_ref[...], axis=-1, keepdims=True), m_ref.shape)

    acc_ref[...] = jnp.zeros(acc_ref.shape, F32)

    def pass2(c, carry):
        k0 = pl.multiple_of(c * SEL_CHUNK, SEL_CHUNK)
        p = jnp.exp2(sc_ref[c] - jnp.concatenate([m_ref[...]] * lane_tiles, axis=1)).astype(BF16)
        acc_ref[...] += jnp.dot(p, vs_ref[pl.ds(k0, SEL_CHUNK), :], preferred_element_type=F32)
        return carry

    lax.fori_loop(0, n_chunks, pass2, 0)

    gate = jax.nn.sigmoid(gate_ref[0])
    for r in range(HEADS_PER_KV):
        o_win = ow_ref[rows[r], :]
        res_s = acc_ref[rows[r], :]
        o_sel = res_s[:, :HEAD_DIM] / res_s[:, HEAD_DIM:]
        g_c = gate[:, r:r + 1]
        g_s = gate[:, HEADS_PER_KV + r:HEADS_PER_KV + r + 1]
        g_w = gate[:, 2 * HEADS_PER_KV + r:2 * HEADS_PER_KV + r + 1]
        o_ref[:, r * HEAD_DIM:(r + 1) * HEAD_DIM] = g_c * o_cmp[r] + g_s * o_sel + g_w * o_win


def _overlap_t(seq):
    ncp = seq // CMP_STRIDE
    nsel = seq // SEL_BLOCK
    c_start = np.arange(ncp) * CMP_STRIDE
    j_start = np.arange(nsel) * SEL_BLOCK
    ov = ((c_start[None, :] < j_start[:, None] + SEL_BLOCK) & (c_start[None, :] + CMP_LEN > j_start[:, None])
          & (np.arange(ncp)[None, :] < ncp - 1))
    return jnp.asarray(ov.astype(np.float32), BF16)


def nsa_attention(q_r, gates_g, kc, vc_aug, ks_r, vs_aug, kw_pad, vw_pad, *, name):
    seq = q_r.shape[0]
    nq = seq // Q_BLOCK
    ncp = seq // CMP_STRIDE
    grp = HEADS_PER_KV * HEAD_DIM
    return pl.pallas_call(
        functools.partial(_nsa_body, seq=seq),
        grid=(N_KV, nq),
        in_specs=[pl.BlockSpec((Q_BLOCK, grp), lambda g, i: (i, g)),
                  pl.BlockSpec((1, Q_BLOCK, HEAD_DIM), lambda g, i: (g, i, 0)),
                  pl.BlockSpec((1, ncp, HEAD_DIM), lambda g, i: (g, 0, 0)),
                  pl.BlockSpec((1, ncp, 2 * HEAD_DIM), lambda g, i: (g, 0, 0)),
                  pl.BlockSpec((seq, HEAD_DIM), lambda g, i: (0, g)),
                  pl.BlockSpec((seq, 2 * HEAD_DIM), lambda g, i: (0, g)),
                  pl.BlockSpec((seq + WINDOW, HEAD_DIM), lambda g, i: (0, g)),
                  pl.BlockSpec((seq + WINDOW, 2 * HEAD_DIM), lambda g, i: (0, g)),
                  pl.BlockSpec((seq // SEL_BLOCK, ncp), lambda g, i: (0, 0))],
        out_specs=pl.BlockSpec((Q_BLOCK, grp), lambda g, i: (i, g)),
        out_shape=jax.ShapeDtypeStruct((seq, ATTN_WIDTH), F32),
        scratch_shapes=[pltpu.VMEM((seq // SEL_CHUNK, HEADS_PER_KV * Q_BLOCK, SEL_CHUNK), F32),
                        pltpu.VMEM((HEADS_PER_KV * Q_BLOCK, HEAD_DIM), F32),
                        pltpu.VMEM((HEADS_PER_KV * Q_BLOCK, 2 * HEAD_DIM), F32),
                        pltpu.VMEM((HEADS_PER_KV * Q_BLOCK, HEAD_DIM), F32)],
        compiler_params=pltpu.CompilerParams(dimension_semantics=("parallel", "arbitrary"),
                                             vmem_limit_bytes=VMEM_LIMIT_BYTES),
        name=name,
    )(q_r, gates_g, kc, vc_aug, ks_r, vs_aug, kw_pad, vw_pad, _overlap_t(seq))


def nsa_mixer(proj, gate_logits, cmp_pos, cmp_w1, cmp_w2, cos, sin_signed, *, tag):
    seq = proj.shape[0]
    q_r, ks_r, vs_aug, kw_r, vw_aug = nsa_prep(proj, cos, sin_signed, name="nsa_prep" + tag)
    kc = nsa_compress(proj, KC_OFF, cos, sin_signed, cmp_pos[0], cmp_w1[0], cmp_w2[0],
                      rope=True, aug=False, name="nsa_cmp_k" + tag)
    vc_aug = nsa_compress(proj, VC_OFF, cos, sin_signed, cmp_pos[1], cmp_w1[1], cmp_w2[1],
                          rope=False, aug=True, name="nsa_cmp_v" + tag)
    kw_pad = jnp.pad(kw_r, ((WINDOW, 0), (0, 0)))
    vw_pad = jnp.pad(vw_aug, ((WINDOW, 0), (0, 0)))
    gl = gate_logits.reshape(seq, 3, N_KV, HEADS_PER_KV).transpose(2, 0, 1, 3).reshape(N_KV, seq, 3 * HEADS_PER_KV)
    gl = jnp.pad(gl, ((0, 0), (0, 0), (0, HEAD_DIM - 3 * HEADS_PER_KV)))
    return nsa_attention(q_r, gl, kc, vc_aug, ks_r, vs_aug, kw_pad, vw_pad, name="nsa_attention" + tag)


LANES = 128
SUBLANES = 8


def _exact_dot(x, m):
    return sum(jnp.dot(part, m, preferred_element_type=F32) for part in _split3(x))


def _exact_dot_left(m, x):
    return sum(jnp.dot(m, part, preferred_element_type=F32) for part in _split3(x))


def _ssd_body(z_ref, xbc_ref, dt_ref, convw_ref, convb_ref, dtb_ref, alog_ref, dskip_ref, normw_ref,
              tril_ref, hexp_ref, o_ref, xcat_ref, act_ref, acsx_ref, dtx_ref, y_ref, state_ref):
    c = pl.program_id(0)
    L = SSD_CHUNK
    half = lax.broadcasted_iota(jnp.int32, (L, LANES), 1) < SSM_HEAD_DIM

    @pl.when(c == 0)
    def _():
        xcat_ref[0:SUBLANES, :] = jnp.zeros((SUBLANES, CONV_DIM), F32)
        state_ref[...] = jnp.zeros(state_ref.shape, F32)

    xcat_ref[SUBLANES:SUBLANES + L, :] = xbc_ref[...]
    slab = 512
    for j in range(CONV_DIM // slab):
        cs = slice(j * slab, (j + 1) * slab)
        conv = convb_ref[:, cs] + xcat_ref[pl.ds(SUBLANES - 3, L), cs] * convw_ref[0:1, cs]
        for k in range(1, CONV_WIDTH):
            conv = conv + xcat_ref[pl.ds(SUBLANES - 3 + k, L), cs] * convw_ref[k:k + 1, cs]
        act_ref[:, cs] = conv * jax.nn.sigmoid(conv)
    xcat_ref[0:SUBLANES, :] = xcat_ref[L:L + SUBLANES, :]

    raw = dt_ref[...] + dtb_ref[...]
    dtv = jnp.maximum(raw, 0.0) + jnp.log1p(jnp.exp(-jnp.abs(raw)))
    a = dtv * (-jnp.exp(alog_ref[...]))
    acs = _exact_dot_left(tril_ref[...], a)
    acs_t = acs.T
    hexp = hexp_ref[...]
    acsx_ref[...] = _exact_dot(acs, hexp)
    dtx_ref[...] = _exact_dot(dtv, hexp)

    tri = lax.broadcasted_iota(jnp.int32, (L, L), 0) >= lax.broadcasted_iota(jnp.int32, (L, L), 1)
    bcol = SSM_WIDTH
    ccol = SSM_WIDTH + SSM_GROUPS * SSM_STATE
    for g in range(SSM_GROUPS):
        bg = act_ref[:, bcol + g * SSM_STATE:bcol + (g + 1) * SSM_STATE]
        cg = act_ref[:, ccol + g * SSM_STATE:ccol + (g + 1) * SSM_STATE].astype(BF16)
        cb = _dot_nt(cg, bg.astype(BF16))
        bg_t = bg.T.astype(BF16)
        for jj in range(2):
            j = 2 * g + jj
            h0, h1 = 2 * j, 2 * j + 1
            ts = slice(j * LANES, (j + 1) * LANES)
            xs = act_ref[:, ts]
            col0 = acsx_ref[:, h0 * LANES:(h0 + 1) * LANES]
            col1 = acsx_ref[:, h1 * LANES:(h1 + 1) * LANES]
            acs_pair = jnp.where(half, col0, col1)
            dt_pair = jnp.where(half, dtx_ref[:, h0 * LANES:(h0 + 1) * LANES], dtx_ref[:, h1 * LANES:(h1 + 1) * LANES])
            xdt = xs * dt_pair
            xdt16 = xdt.astype(BF16)
            zero16 = jnp.zeros_like(xdt16)
            seg0 = jnp.concatenate([col0, col0], axis=1) - acs_t[h0:h0 + 1, :]
            seg1 = jnp.concatenate([col1, col1], axis=1) - acs_t[h1:h1 + 1, :]
            g0 = (cb * jnp.exp(jnp.where(tri, seg0, NEG))).astype(BF16)
            g1 = (cb * jnp.exp(jnp.where(tri, seg1, NEG))).astype(BF16)
            y = (jnp.dot(g0, jnp.where(half, xdt16, zero16), preferred_element_type=F32)
                 + jnp.dot(g1, jnp.where(half, zero16, xdt16), preferred_element_type=F32))
            st = state_ref[j]
            y = y + jnp.dot(cg, st.astype(BF16), preferred_element_type=F32) * jnp.exp(acs_pair)
            last = acs_pair[L - 1:L, :]
            xw = (xdt * jnp.exp(last - acs_pair)).astype(BF16)
            state_ref[j] = jnp.exp(last) * st + jnp.dot(bg_t, xw, preferred_element_type=F32)
            y_ref[:, ts] = y + dskip_ref[:, ts] * xs

    gw = SSM_WIDTH // SSM_GROUPS
    for g in range(SSM_GROUPS):
        gs = slice(g * gw, (g + 1) * gw)
        zz = z_ref[:, gs]
        yz = y_ref[:, gs] * (zz * jax.nn.sigmoid(zz))
        ms = jnp.mean(yz * yz, axis=-1, keepdims=True)
        o_ref[:, gs] = (yz * lax.rsqrt(ms + EPS) * normw_ref[:, gs]).astype(o_ref.dtype)


def ssd_mixer(proj, z_col, xbc_col, small, conv_w, conv_b, dt_bias, a_log, d_skip, norm_w, *, name):
    seq = proj.shape[0]
    L = SSD_CHUNK
    pad = lambda v: jnp.pad(v.astype(F32), (0, LANES - v.shape[0])).reshape(1, LANES)
    tril = jnp.asarray(np.tril(np.ones((L, L), np.float32)), BF16)
    hexp = jnp.asarray((np.arange(LANES)[:, None] == np.arange(SSM_HEADS * LANES)[None, :] // LANES)
                       .astype(np.float32), BF16)
    dskip_x = jnp.repeat(d_skip.astype(F32), SSM_HEAD_DIM).reshape(1, SSM_WIDTH)
    full = lambda shape: pl.BlockSpec(shape, lambda c: (0,) * len(shape))
    return pl.pallas_call(
        _ssd_body,
        grid=(seq // L,),
        in_specs=[pl.BlockSpec((L, SSM_WIDTH), lambda c, o=z_col // SSM_WIDTH: (c, o)),
                  pl.BlockSpec((L, CONV_DIM), lambda c, o=xbc_col // CONV_DIM: (c, o)),
                  pl.BlockSpec((L, LANES), lambda c: (c, 0)),
                  full((CONV_WIDTH, CONV_DIM)), full((1, CONV_DIM)), full((1, LANES)), full((1, LANES)),
                  full((1, SSM_WIDTH)), full((1, SSM_WIDTH)), full((L, L)), full((LANES, SSM_HEADS * LANES))],
        out_specs=pl.BlockSpec((L, SSM_WIDTH), lambda c: (c, 0)),
        out_shape=jax.ShapeDtypeStruct((seq, SSM_WIDTH), BF16),
        scratch_shapes=[pltpu.VMEM((L + SUBLANES, CONV_DIM), F32),
                        pltpu.VMEM((L, CONV_DIM), F32),
                        pltpu.VMEM((L, SSM_HEADS * LANES), F32),
                        pltpu.VMEM((L, SSM_HEADS * LANES), F32),
                        pltpu.VMEM((L, SSM_WIDTH), F32),
                        pltpu.VMEM((SSM_HEADS // 2, SSM_STATE, LANES), F32)],
        compiler_params=pltpu.CompilerParams(dimension_semantics=("arbitrary",),
                                             vmem_limit_bytes=VMEM_LIMIT_BYTES),
        name=name,
    )(proj, proj, small, conv_w, conv_b.reshape(1, CONV_DIM), pad(dt_bias), pad(a_log), dskip_x,
      norm_w.reshape(1, SSM_WIDTH).astype(F32), tril, hexp)


def _rope_tables(seq):
    inv = 1.0 / (ROPE_THETA ** (jnp.arange(0, HEAD_DIM, 2, dtype=F32) / HEAD_DIM))
    ang = jnp.arange(seq, dtype=F32)[:, None] * inv[None, :]
    ang = jnp.concatenate([ang, ang], axis=-1)
    return jnp.cos(ang), jnp.sin(ang)


def kernel(x, norm_mix, w_in, cmp_pos, cmp_w1, cmp_w2, attn_norm, conv_w, conv_b, dt_bias, a_log,
           d_skip, ssm_norm, w_out, norm_mlp, w_up, w_down, norm_final):
    batch, seq, _ = x.shape
    assert batch == 1
    cos, sin = _rope_tables(seq)
    half_sign = jnp.concatenate([-jnp.ones((HEAD_DIM // 2,), F32), jnp.ones((HEAD_DIM // 2,), F32)])
    sin_signed = sin * half_sign
    h = x.reshape(seq, D_MODEL)
    w_ssd = jnp.concatenate([w_in[:, :, W_XBC_OFF:W_DT_OFF], w_in[:, :, W_Z_OFF:W_XBC_OFF]], axis=2).astype(BF16)
    w_small = jnp.concatenate(
        [w_in[:, :, W_DT_OFF:], w_in[:, :, W_GATE_OFF:W_Z_OFF],
         jnp.zeros((DEPTH, D_MODEL, SMALL_COLS - SSM_HEADS - GATE_COLS), F32)], axis=2).astype(BF16)
    for l in range(DEPTH):
        xn = rmsnorm(h, norm_mix[l], out_dtype=BF16, name=f"norm_mix{l}")
        proj_attn = matmul_ws(xn, w_in, l, ATTN_PROJ_COLS, name=f"in_proj_attn{l}")
        proj_ssd = matmul(xn, w_ssd, l, name=f"in_proj_ssd{l}")
        small = matmul(xn, w_small, l, name=f"in_proj_small{l}")
        attn = nsa_mixer(proj_attn, small[:, SSM_HEADS:SSM_HEADS + GATE_COLS], cmp_pos[l], cmp_w1[l], cmp_w2[l],
                         cos, sin_signed, tag=str(l))
        ssm = ssd_mixer(proj_ssd, Z_COL, XBC_COL, small, conv_w[l], conv_b[l], dt_bias[l], a_log[l], d_skip[l],
                        ssm_norm[l], name=f"ssd{l}")
        mix = norm_concat(attn, attn_norm[l], ssm, name=f"norm_concat{l}")
        h = matmul_ws(mix, w_out, l, D_MODEL, residual=h, name=f"out_proj{l}")
        hn = rmsnorm(h, norm_mlp[l], out_dtype=BF16, name=f"norm_mlp{l}")
        u = matmul_ws(hn, w_up, l, D_FF, act="relu2", out_dtype=BF16, name=f"mlp_up{l}")
        h = matmul(u, w_down, l, residual=h, bk=4096, bn=512, name=f"mlp_down{l}")
    out = rmsnorm(h, norm_final, out_dtype=F32, name="norm_final")
    return out.reshape(batch, seq, D_MODEL)
```

```python
import functools

import jax
import jax.numpy as jnp
import numpy as np
from jax import lax
from jax.experimental import pallas as pl
from jax.experimental.pallas import tpu as pltpu

F32 = jnp.float32
BF16 = jnp.bfloat16

D_MODEL = 4096
DEPTH = 2
ATTN_WIDTH = 2048
SSM_WIDTH = 2048
HEAD_DIM = 128
N_HEADS = 16
N_KV = 4
HEADS_PER_KV = 4
KV_WIDTH = 512
ROPE_THETA = 10000.0
CMP_LEN = 32
CMP_STRIDE = 16
SEL_BLOCK = 64
SEL_TOP = 16
WINDOW = 512
Q_BLOCK = 128
BIG = 1e9
SSM_HEAD_DIM = 64
SSM_HEADS = 32
SSM_GROUPS = 8
SSM_STATE = 128
CONV_WIDTH = 4
SSD_CHUNK = 256
CONV_DIM = SSM_WIDTH + 2 * SSM_GROUPS * SSM_STATE
D_FF = 4 * D_MODEL
EPS = 1e-5
NEG = -1e30

VMEM_LIMIT_BYTES = 52 * 1024 * 1024

GATE_COLS = 3 * N_HEADS
W_KV_OFF = ATTN_WIDTH
W_GATE_OFF = W_KV_OFF + 6 * KV_WIDTH
W_Z_OFF = W_GATE_OFF + GATE_COLS
W_XBC_OFF = W_Z_OFF + SSM_WIDTH
W_DT_OFF = W_XBC_OFF + CONV_DIM
Z_COL = ATTN_WIDTH
XBC_COL = Z_COL + SSM_WIDTH
KC_OFF = XBC_COL + CONV_DIM
VC_OFF, KS_OFF, VS_OFF, KW_OFF, VW_OFF = (KC_OFF + i * KV_WIDTH for i in range(1, 6))
SMALL_COLS = 128


def _matmul_body(*refs, nk, act, has_res):
    a_ref, b_ref = refs[0], refs[1]
    r_ref = refs[2] if has_res else None
    o_ref = refs[3] if has_res else refs[2]
    part = jnp.dot(a_ref[...], b_ref[...].astype(BF16), preferred_element_type=F32)

    def finish(acc):
        if act == "relu2":
            acc = jnp.square(jnp.maximum(acc, 0.0))
        if has_res:
            acc = acc + r_ref[...]
        o_ref[...] = acc.astype(o_ref.dtype)

    if nk == 1:
        finish(part)
    else:
        acc_ref = refs[-1]
        k = pl.program_id(2)

        @pl.when(k == 0)
        def _():
            acc_ref[...] = part

        @pl.when(k > 0)
        def _():
            acc_ref[...] += part

        @pl.when(k == nk - 1)
        def _():
            finish(acc_ref[...])


def matmul(a, b, layer, *, residual=None, act=None, out_dtype=F32, bm=1024, bn=1024, bk=None, name):
    m, kdim = a.shape
    _, _, n = b.shape
    bk = kdim if bk is None else bk
    bn = min(bn, n)
    assert m % bm == 0 and n % bn == 0 and kdim % bk == 0
    nk = kdim // bk
    in_specs = [pl.BlockSpec((bm, bk), lambda i, j, k: (i, k)),
                pl.BlockSpec((None, bk, bn), lambda i, j, k: (layer, k, j))]
    args = [a, b]
    if residual is not None:
        in_specs.append(pl.BlockSpec((bm, bn), lambda i, j, k: (i, j)))
        args.append(residual)
    return pl.pallas_call(
        functools.partial(_matmul_body, nk=nk, act=act, has_res=residual is not None),
        grid=(m // bm, n // bn, nk),
        in_specs=in_specs,
        out_specs=pl.BlockSpec((bm, bn), lambda i, j, k: (i, j)),
        out_shape=jax.ShapeDtypeStruct((m, n), out_dtype),
        scratch_shapes=[pltpu.VMEM((bm, bn), F32)] if nk > 1 else [],
        compiler_params=pltpu.CompilerParams(
            dimension_semantics=("parallel", "parallel", "arbitrary"),
            vmem_limit_bytes=VMEM_LIMIT_BYTES),
        name=name,
    )(*args)


def _rmsnorm_body(x_ref, w_ref, o_ref):
    x = x_ref[...]
    y = x * lax.rsqrt(jnp.mean(x * x, axis=-1, keepdims=True) + EPS)
    o_ref[...] = (y * w_ref[...]).astype(o_ref.dtype)


def rmsnorm(x, w, *, out_dtype, rows=256, name):
    m, d = x.shape
    return pl.pallas_call(
        _rmsnorm_body,
        grid=(m // rows,),
        in_specs=[pl.BlockSpec((rows, d), lambda i: (i, 0)),
                  pl.BlockSpec((1, d), lambda i: (0, 0))],
        out_specs=pl.BlockSpec((rows, d), lambda i: (i, 0)),
        out_shape=jax.ShapeDtypeStruct((m, d), out_dtype),
        compiler_params=pltpu.CompilerParams(dimension_semantics=("parallel",)),
        name=name,
    )(x, w.reshape(1, d))


def _norm_concat_body(a_ref, w_ref, s_ref, o_ref):
    x = a_ref[...]
    y = x * lax.rsqrt(jnp.mean(x * x, axis=-1, keepdims=True) + EPS)
    o_ref[:, :ATTN_WIDTH] = (y * w_ref[...]).astype(o_ref.dtype)
    o_ref[:, ATTN_WIDTH:] = s_ref[...]


def norm_concat(attn, w, ssm, *, rows=256, name):
    m = attn.shape[0]
    return pl.pallas_call(
        _norm_concat_body,
        grid=(m // rows,),
        in_specs=[pl.BlockSpec((rows, ATTN_WIDTH), lambda i: (i, 0)),
                  pl.BlockSpec((1, ATTN_WIDTH), lambda i: (0, 0)),
                  pl.BlockSpec((rows, SSM_WIDTH), lambda i: (i, 0))],
        out_specs=pl.BlockSpec((rows, D_MODEL), lambda i: (i, 0)),
        out_shape=jax.ShapeDtypeStruct((m, D_MODEL), BF16),
        compiler_params=pltpu.CompilerParams(dimension_semantics=("parallel",)),
        name=name,
    )(attn, w.reshape(1, ATTN_WIDTH), ssm)


SEL_CHUNK = 1024
SEL_SUB = 256
LOG2_E = 1.4426950408889634


def _split3(x):
    hi = x.astype(BF16)
    r1 = x - hi.astype(F32)
    mid = r1.astype(BF16)
    lo = (r1 - mid.astype(F32)).astype(BF16)
    return hi, mid, lo


def _dot_nt(a, b):
    return lax.dot_general(a, b, (((1,), (1,)), ((), ())), preferred_element_type=F32)


def _prep_body(q_ref, ks_ref, vs_ref, kw_ref, vw_ref, cos_ref, sin_ref, qo, kso, vso, kwo, vwo):
    step = pl.program_id(0)

    @pl.when(step == 0)
    def _():
        kwo[...] = jnp.zeros(kwo.shape, BF16)
        vwo[...] = jnp.zeros(vwo.shape, BF16)

    @pl.when(step > 0)
    def _():
        _prep_rows(q_ref, ks_ref, vs_ref, kw_ref, vw_ref, cos_ref, sin_ref, qo, kso, vso, kwo, vwo)


def _prep_rows(q_ref, ks_ref, vs_ref, kw_ref, vw_ref, cos_ref, sin_ref, qo, kso, vso, kwo, vwo):
    cos = cos_ref[...]
    sin = sin_ref[...]
    scale = HEAD_DIM ** -0.5 * LOG2_E

    def rope(t):
        return t * cos + pltpu.roll(t, HEAD_DIM // 2, 1) * sin

    for h in range(N_HEADS):
        sl = slice(h * HEAD_DIM, (h + 1) * HEAD_DIM)
        qo[:, sl] = (rope(q_ref[:, sl]) * scale).astype(BF16)
    ones = jnp.ones((q_ref.shape[0], HEAD_DIM), BF16)
    for g in range(N_KV):
        sl = slice(g * HEAD_DIM, (g + 1) * HEAD_DIM)
        kso[:, sl] = rope(ks_ref[:, sl]).astype(BF16)
        kwo[:, sl] = rope(kw_ref[:, sl]).astype(BF16)
        vso[:, 2 * g * HEAD_DIM:(2 * g + 1) * HEAD_DIM] = vs_ref[:, sl].astype(BF16)
        vso[:, (2 * g + 1) * HEAD_DIM:(2 * g + 2) * HEAD_DIM] = ones
        vwo[:, 2 * g * HEAD_DIM:(2 * g + 1) * HEAD_DIM] = vw_ref[:, sl].astype(BF16)
        vwo[:, (2 * g + 1) * HEAD_DIM:(2 * g + 2) * HEAD_DIM] = ones


def nsa_prep(proj, cos, sin_signed, *, name):
    seq = proj.shape[0]
    rows = WINDOW
    prev = lambda i: jnp.maximum(i - 1, 0)
    kvb = lambda off: pl.BlockSpec((rows, KV_WIDTH), lambda i, o=off // KV_WIDTH: (prev(i), o))
    return pl.pallas_call(
        _prep_body,
        grid=(seq // rows + 1,),
        in_specs=[pl.BlockSpec((rows, ATTN_WIDTH), lambda i: (prev(i), 0)),
                  kvb(KS_OFF), kvb(VS_OFF), kvb(KW_OFF), kvb(VW_OFF),
                  pl.BlockSpec((rows, HEAD_DIM), lambda i: (prev(i), 0)),
                  pl.BlockSpec((rows, HEAD_DIM), lambda i: (prev(i), 0))],
        out_specs=[pl.BlockSpec((rows, ATTN_WIDTH), lambda i: (prev(i), 0)),
                   pl.BlockSpec((rows, KV_WIDTH), lambda i: (prev(i), 0)),
                   pl.BlockSpec((rows, 2 * KV_WIDTH), lambda i: (prev(i), 0)),
                   pl.BlockSpec((rows, KV_WIDTH), lambda i: (i, 0)),
                   pl.BlockSpec((rows, 2 * KV_WIDTH), lambda i: (i, 0))],
        out_shape=[jax.ShapeDtypeStruct((seq, ATTN_WIDTH), BF16),
                   jax.ShapeDtypeStruct((seq, KV_WIDTH), BF16),
                   jax.ShapeDtypeStruct((seq, 2 * KV_WIDTH), BF16),
                   jax.ShapeDtypeStruct((seq + WINDOW, KV_WIDTH), BF16),
                   jax.ShapeDtypeStruct((seq + WINDOW, 2 * KV_WIDTH), BF16)],
        compiler_params=pltpu.CompilerParams(dimension_semantics=("arbitrary",),
                                             vmem_limit_bytes=VMEM_LIMIT_BYTES),
        name=name,
    )(proj, proj, proj, proj, proj, cos, sin_signed)


def _compress_body(t_ref, cos_ref, sin_ref, pos_ref, w1_ref, w2_ref, o_ref, tr_ref, *, rope, aug):
    seq = t_ref.shape[0]
    ncp = seq // CMP_STRIDE
    t = t_ref[...]
    if rope:
        t = t * cos_ref[...] + pltpu.roll(t, HEAD_DIM // 2, 1) * sin_ref[...]
    tr_ref[...] = t
    acc_a = jnp.zeros((ncp, HEAD_DIM), F32)
    acc_b = jnp.zeros((ncp, HEAD_DIM), F32)
    for p in range(CMP_STRIDE):
        tp = tr_ref[pl.ds(p, ncp, stride=CMP_STRIDE), :]
        wa = w1_ref[p * HEAD_DIM:(p + 1) * HEAD_DIM, :].astype(BF16)
        wb = w1_ref[(CMP_STRIDE + p) * HEAD_DIM:(CMP_STRIDE + p + 1) * HEAD_DIM, :].astype(BF16)
        acc_a += jnp.dot((tp + pos_ref[p:p + 1, :]).astype(BF16), wa, preferred_element_type=F32)
        acc_b += jnp.dot((tp + pos_ref[CMP_STRIDE + p:CMP_STRIDE + p + 1, :]).astype(BF16), wb,
                         preferred_element_type=F32)
    pre = acc_a + pltpu.roll(acc_b, ncp - 1, 0)
    hid = pre * jax.nn.sigmoid(pre)
    out = jnp.dot(hid.astype(BF16), w2_ref[...].astype(BF16), preferred_element_type=F32)
    o_ref[0, :, :HEAD_DIM] = out.astype(BF16)
    if aug:
        o_ref[0, :, HEAD_DIM:] = jnp.ones((ncp, HEAD_DIM), BF16)


def nsa_compress(proj, col_off, cos, sin_signed, pos, w1, w2, *, rope, aug, name):
    seq = proj.shape[0]
    ncp = seq // CMP_STRIDE
    width = 2 * HEAD_DIM if aug else HEAD_DIM
    return pl.pallas_call(
        functools.partial(_compress_body, rope=rope, aug=aug),
        grid=(N_KV,),
        in_specs=[pl.BlockSpec((seq, HEAD_DIM), lambda g, o=col_off // HEAD_DIM: (0, o + g)),
                  pl.BlockSpec((seq, HEAD_DIM), lambda g: (0, 0)),
                  pl.BlockSpec((seq, HEAD_DIM), lambda g: (0, 0)),
                  pl.BlockSpec((CMP_LEN, HEAD_DIM), lambda g: (0, 0)),
                  pl.BlockSpec((CMP_LEN * HEAD_DIM, HEAD_DIM), lambda g: (0, 0)),
                  pl.BlockSpec((HEAD_DIM, HEAD_DIM), lambda g: (0, 0))],
        out_specs=pl.BlockSpec((1, ncp, width), lambda g: (g, 0, 0)),
        out_shape=jax.ShapeDtypeStruct((N_KV, ncp, width), BF16),
        scratch_shapes=[pltpu.VMEM((seq, HEAD_DIM), F32)],
        compiler_params=pltpu.CompilerParams(dimension_semantics=("parallel",),
                                             vmem_limit_bytes=VMEM_LIMIT_BYTES),
        name=name,
    )(proj, cos, sin_signed, pos, w1, w2)


def _nsa_body(q_ref, gate_ref, kc_ref, vc_ref, ks_ref, vs_ref, kw_ref, vw_ref, ovt_ref, o_ref,
              sc_ref, m_ref, acc_ref, ow_ref, *, seq):
    i = pl.program_id(1)
    t0 = i * Q_BLOCK
    ncp = seq // CMP_STRIDE
    nsel = seq // SEL_BLOCK
    top = min(SEL_TOP, nsel)
    nwin = WINDOW + Q_BLOCK
    rows = [slice(r * Q_BLOCK, (r + 1) * Q_BLOCK) for r in range(HEADS_PER_KV)]

    q4 = jnp.concatenate([q_ref[:, r * HEAD_DIM:(r + 1) * HEAD_DIM] for r in range(HEADS_PER_KV)], axis=0)

    s_c = _dot_nt(q4, kc_ref[0])
    c_end = lax.broadcasted_iota(jnp.int32, (Q_BLOCK, ncp), 1) * CMP_STRIDE + (CMP_LEN - 1)
    tq_c = t0 + lax.broadcasted_iota(jnp.int32, (Q_BLOCK, ncp), 0)
    cmask = c_end <= tq_c
    vc = vc_ref[0]
    psum = jnp.zeros((Q_BLOCK, ncp), F32)
    o_cmp = []
    for r in range(HEADS_PER_KV):
        s = jnp.where(cmask, s_c[rows[r]], NEG)
        m = jnp.max(s, axis=-1, keepdims=True)
        p = jnp.where(cmask, jnp.exp2(s - m), 0.0)
        pn = p / jnp.maximum(jnp.sum(p, axis=-1, keepdims=True), 1e-30)
        psum = psum + pn
        o_cmp.append(jnp.dot(pn.astype(BF16), vc[:, :HEAD_DIM], preferred_element_type=F32))

    ovt = ovt_ref[...]
    imp = sum(_dot_nt(ovt, part) for part in _split3(psum))
    blk = lax.broadcasted_iota(jnp.int32, (nsel, Q_BLOCK), 0)
    cur = (t0 + lax.broadcasted_iota(jnp.int32, (nsel, Q_BLOCK), 1)) // SEL_BLOCK
    blkf = blk.astype(F32)
    forced = jnp.where(blk == 0, 1.0, jnp.where(blk == cur, 1.0, jnp.where(blk == cur - 1, 1.0, 0.0)))
    imp = jnp.where(forced > 0.5, -jnp.inf, jnp.where(blk > cur, -BIG, imp))
    sel_t = forced

    def topk_round(imp, sel_t):
        mx = jnp.max(imp, axis=0, keepdims=True)
        first = jnp.min(jnp.where(imp == mx, blkf, float(nsel)), axis=0, keepdims=True)
        hit = blkf == first
        return jnp.where(hit, -jnp.inf, imp), jnp.where(hit, 1.0, sel_t)

    w0 = pl.multiple_of(t0, Q_BLOCK)
    s_w = _dot_nt(q4, kw_ref[pl.ds(w0, nwin), :])
    vwin = vw_ref[pl.ds(w0, nwin), :]
    wpos = t0 - WINDOW + lax.broadcasted_iota(jnp.int32, (Q_BLOCK, nwin), 1)
    tq_w = t0 + lax.broadcasted_iota(jnp.int32, (Q_BLOCK, nwin), 0)
    wvalid = jnp.where(wpos <= tq_w, jnp.where(wpos > tq_w - WINDOW, jnp.where(wpos >= 0, 1.0, 0.0), 0.0), 0.0)
    wmask = wvalid > 0.5
    rounds = top - 3
    per_head = -(-rounds // HEADS_PER_KV)
    for r in range(HEADS_PER_KV):
        sm = jnp.where(wmask, s_w[rows[r]], NEG)
        pw = jnp.exp2(sm - jnp.max(sm, axis=-1, keepdims=True)).astype(BF16)
        res_w = jnp.dot(pw, vwin, preferred_element_type=F32)
        ow_ref[rows[r], :] = res_w[:, :HEAD_DIM] / res_w[:, HEAD_DIM:]
        for _ in range(min(per_head, rounds - r * per_head)):
            imp, sel_t = topk_round(imp, sel_t)
    sel_t = jnp.where(blk <= cur, sel_t, 0.0)
    sel = sel_t.T.astype(BF16)

    n_chunks = (t0 + Q_BLOCK + SEL_CHUNK - 1) // SEL_CHUNK
    diff = (lax.broadcasted_iota(jnp.int32, (nsel, SEL_SUB), 0)
            - lax.broadcasted_iota(jnp.int32, (nsel, SEL_SUB), 1) // SEL_BLOCK)
    lane_k = lax.broadcasted_iota(jnp.int32, (Q_BLOCK, SEL_SUB), 1)
    tq_s = t0 + lax.broadcasted_iota(jnp.int32, (Q_BLOCK, SEL_SUB), 0)
    lane_tiles = SEL_CHUNK // HEAD_DIM
    m_ref[...] = jnp.full(m_ref.shape, NEG, F32)

    def pass1(c, carry):
        best = [m_ref[rows[r], :] for r in range(HEADS_PER_KV)]
        for u in range(SEL_CHUNK // SEL_SUB):
            k0 = pl.multiple_of(c * SEL_CHUNK + u * SEL_SUB, SEL_SUB)
            s = _dot_nt(q4, ks_ref[pl.ds(k0, SEL_SUB), :])
            expand = jnp.where(diff == k0 // SEL_BLOCK, 1.0, 0.0).astype(BF16)
            picked = jnp.dot(sel, expand, preferred_element_type=F32)
            picked = jnp.where(k0 + lane_k <= tq_s, picked, 0.0)
            mask = picked > 0.5
            for r in range(HEADS_PER_KV):
                sm = jnp.where(mask, s[rows[r]], NEG)
                sc_ref[c, rows[r], u * SEL_SUB:(u + 1) * SEL_SUB] = sm
                best[r] = jnp.maximum(best[r], jnp.maximum(sm[:, :HEAD_DIM], sm[:, HEAD_DIM:]))
        for r in range(HEADS_PER_KV):
            m_ref[rows[r], :] = best[r]
        return carry

    lax.fori_loop(0, n_chunks, pass1, 0)
    m_ref[...] = jnp.broadcast_to(jnp.max(m_ref[...], axis=-1, keepdims=True), m_ref.shape)

    acc_ref[...] = jnp.zeros(acc_ref.shape, F32)

    def pass2(c, carry):
        k0 = pl.multiple_of(c * SEL_CHUNK, SEL_CHUNK)
        p = jnp.exp2(sc_ref[c] - jnp.concatenate([m_ref[...]] * lane_tiles, axis=1)).astype(BF16)
        acc_ref[...] += jnp.dot(p, vs_ref[pl.ds(k0, SEL_CHUNK), :], preferred_element_type=F32)
        return carry

    lax.fori_loop(0, n_chunks, pass2, 0)

    gate = jax.nn.sigmoid(gate_ref[0])
    for r in range(HEADS_PER_KV):
        o_win = ow_ref[rows[r], :]
        res_s = acc_ref[rows[r], :]
        o_sel = res_s[:, :HEAD_DIM] / res_s[:, HEAD_DIM:]
        g_c = gate[:, r:r + 1]
        g_s = gate[:, HEADS_PER_KV + r:HEADS_PER_KV + r + 1]
        g_w = gate[:, 2 * HEADS_PER_KV + r:2 * HEADS_PER_KV + r + 1]
        o_ref[:, r * HEAD_DIM:(r + 1) * HEAD_DIM] = g_c * o_cmp[r] + g_s * o_sel + g_w * o_win


def _overlap_t(seq):
    ncp = seq // CMP_STRIDE
    nsel = seq // SEL_BLOCK
    c_start = np.arange(ncp) * CMP_STRIDE
    j_start = np.arange(nsel) * SEL_BLOCK
    ov = ((c_start[None, :] < j_start[:, None] + SEL_BLOCK) & (c_start[None, :] + CMP_LEN > j_start[:, None])
          & (np.arange(ncp)[None, :] < ncp - 1))
    return jnp.asarray(ov.astype(np.float32), BF16)


def nsa_attention(q_r, gates_g, kc, vc_aug, ks_r, vs_aug, kw_pad, vw_pad, *, name):
    seq = q_r.shape[0]
    nq = seq // Q_BLOCK
    ncp = seq // CMP_STRIDE
    grp = HEADS_PER_KV * HEAD_DIM
    return pl.pallas_call(
        functools.partial(_nsa_body, seq=seq),
        grid=(N_KV, nq),
        in_specs=[pl.BlockSpec((Q_BLOCK, grp), lambda g, i: (i, g)),
                  pl.BlockSpec((1, Q_BLOCK, HEAD_DIM), lambda g, i: (g, i, 0)),
                  pl.BlockSpec((1, ncp, HEAD_DIM), lambda g, i: (g, 0, 0)),
                  pl.BlockSpec((1, ncp, 2 * HEAD_DIM), lambda g, i: (g, 0, 0)),
                  pl.BlockSpec((seq, HEAD_DIM), lambda g, i: (0, g)),
                  pl.BlockSpec((seq, 2 * HEAD_DIM), lambda g, i: (0, g)),
                  pl.BlockSpec((seq + WINDOW, HEAD_DIM), lambda g, i: (0, g)),
                  pl.BlockSpec((seq + WINDOW, 2 * HEAD_DIM), lambda g, i: (0, g)),
                  pl.BlockSpec((seq // SEL_BLOCK, ncp), lambda g, i: (0, 0))],
        out_specs=pl.BlockSpec((Q_BLOCK, grp), lambda g, i: (i, g)),
        out_shape=jax.ShapeDtypeStruct((seq, ATTN_WIDTH), F32),
        scratch_shapes=[pltpu.VMEM((seq // SEL_CHUNK, HEADS_PER_KV * Q_BLOCK, SEL_CHUNK), F32),
                        pltpu.VMEM((HEADS_PER_KV * Q_BLOCK, HEAD_DIM), F32),
                        pltpu.VMEM((HEADS_PER_KV * Q_BLOCK, 2 * HEAD_DIM), F32),
                        pltpu.VMEM((HEADS_PER_KV * Q_BLOCK, HEAD_DIM), F32)],
        compiler_params=pltpu.CompilerParams(dimension_semantics=("parallel", "arbitrary"),
                                             vmem_limit_bytes=VMEM_LIMIT_BYTES),
        name=name,
    )(q_r, gates_g, kc, vc_aug, ks_r, vs_aug, kw_pad, vw_pad, _overlap_t(seq))


def nsa_mixer(proj, gate_logits, cmp_pos, cmp_w1, cmp_w2, cos, sin_signed, *, tag):
    seq = proj.shape[0]
    q_r, ks_r, vs_aug, kw_pad, vw_pad = nsa_prep(proj, cos, sin_signed, name="nsa_prep" + tag)
    kc = nsa_compress(proj, KC_OFF, cos, sin_signed, cmp_pos[0], cmp_w1[0], cmp_w2[0],
                      rope=True, aug=False, name="nsa_cmp_k" + tag)
    vc_aug = nsa_compress(proj, VC_OFF, cos, sin_signed, cmp_pos[1], cmp_w1[1], cmp_w2[1],
                          rope=False, aug=True, name="nsa_cmp_v" + tag)
    gl = gate_logits.reshape(seq, 3, N_KV, HEADS_PER_KV).transpose(2, 0, 1, 3).reshape(N_KV, seq, 3 * HEADS_PER_KV)
    gl = jnp.pad(gl, ((0, 0), (0, 0), (0, HEAD_DIM - 3 * HEADS_PER_KV)))
    return nsa_attention(q_r, gl, kc, vc_aug, ks_r, vs_aug, kw_pad, vw_pad, name="nsa_attention" + tag)


LANES = 128
SUBLANES = 8


def _exact_dot(x, m):
    return sum(jnp.dot(part, m, preferred_element_type=F32) for part in _split3(x))


def _exact_dot_left(m, x):
    return sum(jnp.dot(m, part, preferred_element_type=F32) for part in _split3(x))


def _ssd_body(z_ref, xbc_ref, dt_ref, convw_ref, convb_ref, dtb_ref, alog_ref, dskip_ref, normw_ref,
              tril_ref, hexp_ref, o_ref, xcat_ref, act_ref, acsx_ref, dtx_ref, y_ref, state_ref):
    c = pl.program_id(0)
    L = SSD_CHUNK
    half = lax.broadcasted_iota(jnp.int32, (L, LANES), 1) < SSM_HEAD_DIM

    @pl.when(c == 0)
    def _():
        xcat_ref[0:SUBLANES, :] = jnp.zeros((SUBLANES, CONV_DIM), F32)
        state_ref[...] = jnp.zeros(state_ref.shape, F32)

    xcat_ref[SUBLANES:SUBLANES + L, :] = xbc_ref[...]
    slab = 512
    for j in range(CONV_DIM // slab):
        cs = slice(j * slab, (j + 1) * slab)
        conv = convb_ref[:, cs] + xcat_ref[pl.ds(SUBLANES - 3, L), cs] * convw_ref[0:1, cs]
        for k in range(1, CONV_WIDTH):
            conv = conv + xcat_ref[pl.ds(SUBLANES - 3 + k, L), cs] * convw_ref[k:k + 1, cs]
        act_ref[:, cs] = conv * jax.nn.sigmoid(conv)
    xcat_ref[0:SUBLANES, :] = xcat_ref[L:L + SUBLANES, :]

    raw = dt_ref[...] + dtb_ref[...]
    dtv = jnp.maximum(raw, 0.0) + jnp.log1p(jnp.exp(-jnp.abs(raw)))
    a = dtv * (-jnp.exp(alog_ref[...]) * LOG2_E)
    acs = _exact_dot_left(tril_ref[...], a)
    acs_t = acs.T
    hexp = hexp_ref[...]
    acsx_ref[...] = _exact_dot(acs, hexp)
    dtx_ref[...] = _exact_dot(dtv, hexp)

    tri = lax.broadcasted_iota(jnp.int32, (L, L), 0) >= lax.broadcasted_iota(jnp.int32, (L, L), 1)
    bcol = SSM_WIDTH
    ccol = SSM_WIDTH + SSM_GROUPS * SSM_STATE
    for g in range(SSM_GROUPS):
        bg = act_ref[:, bcol + g * SSM_STATE:bcol + (g + 1) * SSM_STATE]
        cg = act_ref[:, ccol + g * SSM_STATE:ccol + (g + 1) * SSM_STATE].astype(BF16)
        cb = _dot_nt(cg, bg.astype(BF16))
        bg_t = bg.T.astype(BF16)
        for jj in range(2):
            j = 2 * g + jj
            h0, h1 = 2 * j, 2 * j + 1
            ts = slice(j * LANES, (j + 1) * LANES)
            xs = act_ref[:, ts]
            col0 = acsx_ref[:, h0 * LANES:(h0 + 1) * LANES]
            col1 = acsx_ref[:, h1 * LANES:(h1 + 1) * LANES]
            acs_pair = jnp.where(half, col0, col1)
            dt_pair = jnp.where(half, dtx_ref[:, h0 * LANES:(h0 + 1) * LANES], dtx_ref[:, h1 * LANES:(h1 + 1) * LANES])
            xdt = xs * dt_pair
            xdt16 = xdt.astype(BF16)
            zero16 = jnp.zeros_like(xdt16)
            seg0 = jnp.concatenate([col0, col0], axis=1) - acs_t[h0:h0 + 1, :]
            seg1 = jnp.concatenate([col1, col1], axis=1) - acs_t[h1:h1 + 1, :]
            g0 = (cb * jnp.exp2(jnp.where(tri, seg0, NEG))).astype(BF16)
            g1 = (cb * jnp.exp2(jnp.where(tri, seg1, NEG))).astype(BF16)
            y = (jnp.dot(g0, jnp.where(half, xdt16, zero16), preferred_element_type=F32)
                 + jnp.dot(g1, jnp.where(half, zero16, xdt16), preferred_element_type=F32))
            st = state_ref[j]
            y = y + jnp.dot(cg, st.astype(BF16), preferred_element_type=F32) * jnp.exp2(acs_pair)
            last = acs_pair[L - 1:L, :]
            xw = (xdt * jnp.exp2(last - acs_pair)).astype(BF16)
            state_ref[j] = jnp.exp2(last) * st + jnp.dot(bg_t, xw, preferred_element_type=F32)
            y_ref[:, ts] = y + dskip_ref[:, ts] * xs

    gw = SSM_WIDTH // SSM_GROUPS
    for g in range(SSM_GROUPS):
        gs = slice(g * gw, (g + 1) * gw)
        zz = z_ref[:, gs]
        yz = y_ref[:, gs] * (zz * jax.nn.sigmoid(zz))
        ms = jnp.mean(yz * yz, axis=-1, keepdims=True)
        o_ref[:, gs] = (yz * lax.rsqrt(ms + EPS) * normw_ref[:, gs]).astype(o_ref.dtype)


def ssd_mixer(proj, z_col, xbc_col, small, conv_w, conv_b, dt_bias, a_log, d_skip, norm_w, *, name):
    seq = proj.shape[0]
    L = SSD_CHUNK
    pad = lambda v: jnp.pad(v.astype(F32), (0, LANES - v.shape[0])).reshape(1, LANES)
    tril = jnp.asarray(np.tril(np.ones((L, L), np.float32)), BF16)
    hexp = jnp.asarray((np.arange(LANES)[:, None] == np.arange(SSM_HEADS * LANES)[None, :] // LANES)
                       .astype(np.float32), BF16)
    dskip_x = jnp.repeat(d_skip.astype(F32), SSM_HEAD_DIM).reshape(1, SSM_WIDTH)
    full = lambda shape: pl.BlockSpec(shape, lambda c: (0,) * len(shape))
    return pl.pallas_call(
        _ssd_body,
        grid=(seq // L,),
        in_specs=[pl.BlockSpec((L, SSM_WIDTH), lambda c, o=z_col // SSM_WIDTH: (c, o)),
                  pl.BlockSpec((L, CONV_DIM), lambda c, o=xbc_col // CONV_DIM: (c, o)),
                  pl.BlockSpec((L, LANES), lambda c: (c, 0)),
                  full((CONV_WIDTH, CONV_DIM)), full((1, CONV_DIM)), full((1, LANES)), full((1, LANES)),
                  full((1, SSM_WIDTH)), full((1, SSM_WIDTH)), full((L, L)), full((LANES, SSM_HEADS * LANES))],
        out_specs=pl.BlockSpec((L, SSM_WIDTH), lambda c: (c, 0)),
        out_shape=jax.ShapeDtypeStruct((seq, SSM_WIDTH), BF16),
        scratch_shapes=[pltpu.VMEM((L + SUBLANES, CONV_DIM), F32),
                        pltpu.VMEM((L, CONV_DIM), F32),
                        pltpu.VMEM((L, SSM_HEADS * LANES), F32),
                        pltpu.VMEM((L, SSM_HEADS * LANES), F32),
                        pltpu.VMEM((L, SSM_WIDTH), F32),
                        pltpu.VMEM((SSM_HEADS // 2, SSM_STATE, LANES), F32)],
        compiler_params=pltpu.CompilerParams(dimension_semantics=("arbitrary",),
                                             vmem_limit_bytes=VMEM_LIMIT_BYTES),
        name=name,
    )(proj, proj, small, conv_w, conv_b.reshape(1, CONV_DIM), pad(dt_bias), pad(a_log), dskip_x,
      norm_w.reshape(1, SSM_WIDTH).astype(F32), tril, hexp)


def _rope_tables(seq):
    inv = 1.0 / (ROPE_THETA ** (jnp.arange(0, HEAD_DIM, 2, dtype=F32) / HEAD_DIM))
    ang = jnp.arange(seq, dtype=F32)[:, None] * inv[None, :]
    ang = jnp.concatenate([ang, ang], axis=-1)
    return jnp.cos(ang), jnp.sin(ang)


def kernel(x, norm_mix, w_in, cmp_pos, cmp_w1, cmp_w2, attn_norm, conv_w, conv_b, dt_bias, a_log,
           d_skip, ssm_norm, w_out, norm_mlp, w_up, w_down, norm_final):
    batch, seq, _ = x.shape
    assert batch == 1
    cos, sin = _rope_tables(seq)
    half_sign = jnp.concatenate([-jnp.ones((HEAD_DIM // 2,), F32), jnp.ones((HEAD_DIM // 2,), F32)])
    sin_signed = sin * half_sign
    h = x.reshape(seq, D_MODEL)
    w_big = jnp.concatenate([w_in[:, :, :ATTN_WIDTH], w_in[:, :, W_Z_OFF:W_DT_OFF],
                             w_in[:, :, W_KV_OFF:W_GATE_OFF]], axis=2).astype(BF16)
    w_small = jnp.concatenate(
        [w_in[:, :, W_DT_OFF:], w_in[:, :, W_GATE_OFF:W_Z_OFF],
         jnp.zeros((DEPTH, D_MODEL, SMALL_COLS - SSM_HEADS - GATE_COLS), F32)], axis=2).astype(BF16)
    for l in range(DEPTH):
        xn = rmsnorm(h, norm_mix[l], out_dtype=BF16, name=f"norm_mix{l}")
        proj = matmul(xn, w_big, l, name=f"in_proj{l}")
        small = matmul(xn, w_small, l, name=f"in_proj_small{l}")
        attn = nsa_mixer(proj, small[:, SSM_HEADS:SSM_HEADS + GATE_COLS], cmp_pos[l], cmp_w1[l], cmp_w2[l],
                         cos, sin_signed, tag=str(l))
        ssm = ssd_mixer(proj, Z_COL, XBC_COL, small, conv_w[l], conv_b[l], dt_bias[l], a_log[l], d_skip[l],
                        ssm_norm[l], name=f"ssd{l}")
        mix = norm_concat(attn, attn_norm[l], ssm, name=f"norm_concat{l}")
        h = matmul(mix, w_out, l, residual=h, bn=512, name=f"out_proj{l}")
        hn = rmsnorm(h, norm_mlp[l], out_dtype=BF16, name=f"norm_mlp{l}")
        u = matmul(hn, w_up, l, act="relu2", out_dtype=BF16, bn=512, name=f"mlp_up{l}")
        h = matmul(u, w_down, l, residual=h, bk=4096, bn=512, name=f"mlp_down{l}")
    out = rmsnorm(h, norm_final, out_dtype=F32, name="norm_final")
    return out.reshape(batch, seq, D_MODEL)
```

```python
import functools

import jax
import jax.numpy as jnp
import numpy as np
from jax import lax
from jax.experimental import pallas as pl
from jax.experimental.pallas import tpu as pltpu

F32 = jnp.float32
BF16 = jnp.bfloat16

D_MODEL = 4096
DEPTH = 2
ATTN_WIDTH = 2048
SSM_WIDTH = 2048
HEAD_DIM = 128
N_HEADS = 16
N_KV = 4
HEADS_PER_KV = 4
KV_WIDTH = 512
ROPE_THETA = 10000.0
CMP_LEN = 32
CMP_STRIDE = 16
SEL_BLOCK = 64
SEL_TOP = 16
WINDOW = 512
Q_BLOCK = 128
BIG = 1e9
SSM_HEAD_DIM = 64
SSM_HEADS = 32
SSM_GROUPS = 8
SSM_STATE = 128
CONV_WIDTH = 4
SSD_CHUNK = 256
CONV_DIM = SSM_WIDTH + 2 * SSM_GROUPS * SSM_STATE
D_FF = 4 * D_MODEL
EPS = 1e-5
NEG = -1e30

VMEM_LIMIT_BYTES = 52 * 1024 * 1024

GATE_COLS = 3 * N_HEADS
W_KV_OFF = ATTN_WIDTH
W_GATE_OFF = W_KV_OFF + 6 * KV_WIDTH
W_Z_OFF = W_GATE_OFF + GATE_COLS
W_XBC_OFF = W_Z_OFF + SSM_WIDTH
W_DT_OFF = W_XBC_OFF + CONV_DIM
Z_COL = ATTN_WIDTH
XBC_COL = Z_COL + SSM_WIDTH
KC_OFF = XBC_COL + CONV_DIM
VC_OFF, KS_OFF, VS_OFF, KW_OFF, VW_OFF = (KC_OFF + i * KV_WIDTH for i in range(1, 6))
SMALL_COLS = 128


def _matmul_body(*refs, nk, act, has_res):
    a_ref, b_ref = refs[0], refs[1]
    r_ref = refs[2] if has_res else None
    o_ref = refs[3] if has_res else refs[2]
    part = jnp.dot(a_ref[...], b_ref[...].astype(BF16), preferred_element_type=F32)

    def finish(acc):
        if act == "relu2":
            acc = jnp.square(jnp.maximum(acc, 0.0))
        if has_res:
            acc = acc + r_ref[...]
        o_ref[...] = acc.astype(o_ref.dtype)

    if nk == 1:
        finish(part)
    else:
        acc_ref = refs[-1]
        k = pl.program_id(2)

        @pl.when(k == 0)
        def _():
            acc_ref[...] = part

        @pl.when(k > 0)
        def _():
            acc_ref[...] += part

        @pl.when(k == nk - 1)
        def _():
            finish(acc_ref[...])


def matmul(a, b, layer, *, residual=None, act=None, out_dtype=F32, bm=1024, bn=1024, bk=None, name):
    m, kdim = a.shape
    _, _, n = b.shape
    bk = kdim if bk is None else bk
    bn = min(bn, n)
    assert m % bm == 0 and n % bn == 0 and kdim % bk == 0
    nk = kdim // bk
    in_specs = [pl.BlockSpec((bm, bk), lambda i, j, k: (i, k)),
                pl.BlockSpec((None, bk, bn), lambda i, j, k: (layer, k, j))]
    args = [a, b]
    if residual is not None:
        in_specs.append(pl.BlockSpec((bm, bn), lambda i, j, k: (i, j)))
        args.append(residual)
    return pl.pallas_call(
        functools.partial(_matmul_body, nk=nk, act=act, has_res=residual is not None),
        grid=(m // bm, n // bn, nk),
        in_specs=in_specs,
        out_specs=pl.BlockSpec((bm, bn), lambda i, j, k: (i, j)),
        out_shape=jax.ShapeDtypeStruct((m, n), out_dtype),
        scratch_shapes=[pltpu.VMEM((bm, bn), F32)] if nk > 1 else [],
        compiler_params=pltpu.CompilerParams(
            dimension_semantics=("parallel", "parallel", "arbitrary"),
            vmem_limit_bytes=VMEM_LIMIT_BYTES),
        name=name,
    )(*args)


def _reorder_w_in_body(w_ref, o_ref):
    o_ref[:, :Z_COL] = w_ref[:, :W_KV_OFF].astype(BF16)
    o_ref[:, Z_COL:KC_OFF] = w_ref[:, W_Z_OFF:W_DT_OFF].astype(BF16)
    o_ref[:, KC_OFF:] = w_ref[:, W_KV_OFF:W_GATE_OFF].astype(BF16)


def reorder_w_in(w_in, *, rows=256):
    depth, d, cols = w_in.shape
    wide = KC_OFF + 6 * KV_WIDTH
    return pl.pallas_call(
        _reorder_w_in_body,
        grid=(depth, d // rows),
        in_specs=[pl.BlockSpec((None, rows, cols), lambda l, i: (l, i, 0))],
        out_specs=pl.BlockSpec((None, rows, wide), lambda l, i: (l, i, 0)),
        out_shape=jax.ShapeDtypeStruct((depth, d, wide), BF16),
        compiler_params=pltpu.CompilerParams(dimension_semantics=("parallel", "parallel"),
                                             vmem_limit_bytes=VMEM_LIMIT_BYTES),
        name="reorder_w_in",
    )(w_in)


def _rmsnorm_body(x_ref, w_ref, o_ref):
    x = x_ref[...]
    y = x * lax.rsqrt(jnp.mean(x * x, axis=-1, keepdims=True) + EPS)
    o_ref[...] = (y * w_ref[...]).astype(o_ref.dtype)


def rmsnorm(x, w, *, out_dtype, rows=256, name):
    m, d = x.shape
    return pl.pallas_call(
        _rmsnorm_body,
        grid=(m // rows,),
        in_specs=[pl.BlockSpec((rows, d), lambda i: (i, 0)),
                  pl.BlockSpec((1, d), lambda i: (0, 0))],
        out_specs=pl.BlockSpec((rows, d), lambda i: (i, 0)),
        out_shape=jax.ShapeDtypeStruct((m, d), out_dtype),
        compiler_params=pltpu.CompilerParams(dimension_semantics=("parallel",)),
        name=name,
    )(x, w.reshape(1, d))


def _norm_concat_body(a_ref, w_ref, s_ref, o_ref):
    x = a_ref[...]
    y = x * lax.rsqrt(jnp.mean(x * x, axis=-1, keepdims=True) + EPS)
    o_ref[:, :ATTN_WIDTH] = (y * w_ref[...]).astype(o_ref.dtype)
    o_ref[:, ATTN_WIDTH:] = s_ref[...]


def norm_concat(attn, w, ssm, *, rows=256, name):
    m = attn.shape[0]
    return pl.pallas_call(
        _norm_concat_body,
        grid=(m // rows,),
        in_specs=[pl.BlockSpec((rows, ATTN_WIDTH), lambda i: (i, 0)),
                  pl.BlockSpec((1, ATTN_WIDTH), lambda i: (0, 0)),
                  pl.BlockSpec((rows, SSM_WIDTH), lambda i: (i, 0))],
        out_specs=pl.BlockSpec((rows, D_MODEL), lambda i: (i, 0)),
        out_shape=jax.ShapeDtypeStruct((m, D_MODEL), BF16),
        compiler_params=pltpu.CompilerParams(dimension_semantics=("parallel",)),
        name=name,
    )(attn, w.reshape(1, ATTN_WIDTH), ssm)


SEL_CHUNK = 1024
SEL_SUB = 256
LOG2_E = 1.4426950408889634


def _split3(x):
    hi = x.astype(BF16)
    r1 = x - hi.astype(F32)
    mid = r1.astype(BF16)
    lo = (r1 - mid.astype(F32)).astype(BF16)
    return hi, mid, lo


def _dot_nt(a, b):
    return lax.dot_general(a, b, (((1,), (1,)), ((), ())), preferred_element_type=F32)


def _prep_body(q_ref, ks_ref, vs_ref, kw_ref, vw_ref, cos_ref, sin_ref, qo, kso, vso, kwo, vwo):
    step = pl.program_id(0)

    @pl.when(step == 0)
    def _():
        kwo[...] = jnp.zeros(kwo.shape, BF16)
        vwo[...] = jnp.zeros(vwo.shape, BF16)

    @pl.when(step > 0)
    def _():
        _prep_rows(q_ref, ks_ref, vs_ref, kw_ref, vw_ref, cos_ref, sin_ref, qo, kso, vso, kwo, vwo)


def _prep_rows(q_ref, ks_ref, vs_ref, kw_ref, vw_ref, cos_ref, sin_ref, qo, kso, vso, kwo, vwo):
    cos = cos_ref[...]
    sin = sin_ref[...]
    scale = HEAD_DIM ** -0.5 * LOG2_E

    def rope(t):
        return t * cos + pltpu.roll(t, HEAD_DIM // 2, 1) * sin

    for h in range(N_HEADS):
        sl = slice(h * HEAD_DIM, (h + 1) * HEAD_DIM)
        qo[:, sl] = (rope(q_ref[:, sl]) * scale).astype(BF16)
    ones = jnp.ones((q_ref.shape[0], HEAD_DIM), BF16)
    for g in range(N_KV):
        sl = slice(g * HEAD_DIM, (g + 1) * HEAD_DIM)
        kso[:, sl] = rope(ks_ref[:, sl]).astype(BF16)
        kwo[:, sl] = rope(kw_ref[:, sl]).astype(BF16)
        vso[:, 2 * g * HEAD_DIM:(2 * g + 1) * HEAD_DIM] = vs_ref[:, sl].astype(BF16)
        vso[:, (2 * g + 1) * HEAD_DIM:(2 * g + 2) * HEAD_DIM] = ones
        vwo[:, 2 * g * HEAD_DIM:(2 * g + 1) * HEAD_DIM] = vw_ref[:, sl].astype(BF16)
        vwo[:, (2 * g + 1) * HEAD_DIM:(2 * g + 2) * HEAD_DIM] = ones


def nsa_prep(proj, cos, sin_signed, *, name):
    seq = proj.shape[0]
    rows = WINDOW
    prev = lambda i: jnp.maximum(i - 1, 0)
    kvb = lambda off: pl.BlockSpec((rows, KV_WIDTH), lambda i, o=off // KV_WIDTH: (prev(i), o))
    return pl.pallas_call(
        _prep_body,
        grid=(seq // rows + 1,),
        in_specs=[pl.BlockSpec((rows, ATTN_WIDTH), lambda i: (prev(i), 0)),
                  kvb(KS_OFF), kvb(VS_OFF), kvb(KW_OFF), kvb(VW_OFF),
                  pl.BlockSpec((rows, HEAD_DIM), lambda i: (prev(i), 0)),
                  pl.BlockSpec((rows, HEAD_DIM), lambda i: (prev(i), 0))],
        out_specs=[pl.BlockSpec((rows, ATTN_WIDTH), lambda i: (prev(i), 0)),
                   pl.BlockSpec((rows, KV_WIDTH), lambda i: (prev(i), 0)),
                   pl.BlockSpec((rows, 2 * KV_WIDTH), lambda i: (prev(i), 0)),
                   pl.BlockSpec((rows, KV_WIDTH), lambda i: (i, 0)),
                   pl.BlockSpec((rows, 2 * KV_WIDTH), lambda i: (i, 0))],
        out_shape=[jax.ShapeDtypeStruct((seq, ATTN_WIDTH), BF16),
                   jax.ShapeDtypeStruct((seq, KV_WIDTH), BF16),
                   jax.ShapeDtypeStruct((seq, 2 * KV_WIDTH), BF16),
                   jax.ShapeDtypeStruct((seq + WINDOW, KV_WIDTH), BF16),
                   jax.ShapeDtypeStruct((seq + WINDOW, 2 * KV_WIDTH), BF16)],
        compiler_params=pltpu.CompilerParams(dimension_semantics=("arbitrary",),
                                             vmem_limit_bytes=VMEM_LIMIT_BYTES),
        name=name,
    )(proj, proj, proj, proj, proj, cos, sin_signed)


def _compress_body(t_ref, cos_ref, sin_ref, pos_ref, w1_ref, w2_ref, o_ref, tr_ref, *, rope, aug):
    seq = t_ref.shape[0]
    ncp = seq // CMP_STRIDE
    t = t_ref[...]
    if rope:
        t = t * cos_ref[...] + pltpu.roll(t, HEAD_DIM // 2, 1) * sin_ref[...]
    tr_ref[...] = t
    acc_a = jnp.zeros((ncp, HEAD_DIM), F32)
    acc_b = jnp.zeros((ncp, HEAD_DIM), F32)
    for p in range(CMP_STRIDE):
        tp = tr_ref[pl.ds(p, ncp, stride=CMP_STRIDE), :]
        wa = w1_ref[p * HEAD_DIM:(p + 1) * HEAD_DIM, :].astype(BF16)
        wb = w1_ref[(CMP_STRIDE + p) * HEAD_DIM:(CMP_STRIDE + p + 1) * HEAD_DIM, :].astype(BF16)
        acc_a += jnp.dot((tp + pos_ref[p:p + 1, :]).astype(BF16), wa, preferred_element_type=F32)
        acc_b += jnp.dot((tp + pos_ref[CMP_STRIDE + p:CMP_STRIDE + p + 1, :]).astype(BF16), wb,
                         preferred_element_type=F32)
    pre = acc_a + pltpu.roll(acc_b, ncp - 1, 0)
    hid = pre * jax.nn.sigmoid(pre)
    out = jnp.dot(hid.astype(BF16), w2_ref[...].astype(BF16), preferred_element_type=F32)
    o_ref[0, :, :HEAD_DIM] = out.astype(BF16)
    if aug:
        o_ref[0, :, HEAD_DIM:] = jnp.ones((ncp, HEAD_DIM), BF16)


def nsa_compress(proj, col_off, cos, sin_signed, pos, w1, w2, *, rope, aug, name):
    seq = proj.shape[0]
    ncp = seq // CMP_STRIDE
    width = 2 * HEAD_DIM if aug else HEAD_DIM
    return pl.pallas_call(
        functools.partial(_compress_body, rope=rope, aug=aug),
        grid=(N_KV,),
        in_specs=[pl.BlockSpec((seq, HEAD_DIM), lambda g, o=col_off // HEAD_DIM: (0, o + g)),
                  pl.BlockSpec((seq, HEAD_DIM), lambda g: (0, 0)),
                  pl.BlockSpec((seq, HEAD_DIM), lambda g: (0, 0)),
                  pl.BlockSpec((CMP_LEN, HEAD_DIM), lambda g: (0, 0)),
                  pl.BlockSpec((CMP_LEN * HEAD_DIM, HEAD_DIM), lambda g: (0, 0)),
                  pl.BlockSpec((HEAD_DIM, HEAD_DIM), lambda g: (0, 0))],
        out_specs=pl.BlockSpec((1, ncp, width), lambda g: (g, 0, 0)),
        out_shape=jax.ShapeDtypeStruct((N_KV, ncp, width), BF16),
        scratch_shapes=[pltpu.VMEM((seq, HEAD_DIM), F32)],
        compiler_params=pltpu.CompilerParams(dimension_semantics=("parallel",),
                                             vmem_limit_bytes=VMEM_LIMIT_BYTES),
        name=name,
    )(proj, cos, sin_signed, pos, w1, w2)


def _nsa_body(q_ref, gate_ref, kc_ref, vc_ref, ks_ref, vs_ref, kw_ref, vw_ref, ovt_ref, o_ref,
              sc_ref, m_ref, acc_ref, ow_ref, *, seq):
    i = pl.program_id(1)
    t0 = i * Q_BLOCK
    ncp = seq // CMP_STRIDE
    nsel = seq // SEL_BLOCK
    top = min(SEL_TOP, nsel)
    nwin = WINDOW + Q_BLOCK
    rows = [slice(r * Q_BLOCK, (r + 1) * Q_BLOCK) for r in range(HEADS_PER_KV)]

    q4 = jnp.concatenate([q_ref[:, r * HEAD_DIM:(r + 1) * HEAD_DIM] for r in range(HEADS_PER_KV)], axis=0)

    s_c = _dot_nt(q4, kc_ref[0])
    c_end = lax.broadcasted_iota(jnp.int32, (Q_BLOCK, ncp), 1) * CMP_STRIDE + (CMP_LEN - 1)
    tq_c = t0 + lax.broadcasted_iota(jnp.int32, (Q_BLOCK, ncp), 0)
    cmask = c_end <= tq_c
    vc = vc_ref[0]
    psum = jnp.zeros((Q_BLOCK, ncp), F32)
    o_cmp = []
    for r in range(HEADS_PER_KV):
        s = jnp.where(cmask, s_c[rows[r]], NEG)
        m = jnp.max(s, axis=-1, keepdims=True)
        p = jnp.where(cmask, jnp.exp2(s - m), 0.0)
        pn = p / jnp.maximum(jnp.sum(p, axis=-1, keepdims=True), 1e-30)
        psum = psum + pn
        o_cmp.append(jnp.dot(pn.astype(BF16), vc[:, :HEAD_DIM], preferred_element_type=F32))

    ovt = ovt_ref[...]
    imp = sum(_dot_nt(ovt, part) for part in _split3(psum))
    blk = lax.broadcasted_iota(jnp.int32, (nsel, Q_BLOCK), 0)
    cur = (t0 + lax.broadcasted_iota(jnp.int32, (nsel, Q_BLOCK), 1)) // SEL_BLOCK
    blkf = blk.astype(F32)
    forced = jnp.where(blk == 0, 1.0, jnp.where(blk == cur, 1.0, jnp.where(blk == cur - 1, 1.0, 0.0)))
    imp = jnp.where(forced > 0.5, -jnp.inf, jnp.where(blk > cur, -BIG, imp))
    sel_t = forced

    def topk_round(imp, sel_t):
        mx = jnp.max(imp, axis=0, keepdims=True)
        first = jnp.min(jnp.where(imp == mx, blkf, float(nsel)), axis=0, keepdims=True)
        hit = blkf == first
        return jnp.where(hit, -jnp.inf, imp), jnp.where(hit, 1.0, sel_t)

    w0 = pl.multiple_of(t0, Q_BLOCK)
    s_w = _dot_nt(q4, kw_ref[pl.ds(w0, nwin), :])
    vwin = vw_ref[pl.ds(w0, nwin), :]
    wpos = t0 - WINDOW + lax.broadcasted_iota(jnp.int32, (Q_BLOCK, nwin), 1)
    tq_w = t0 + lax.broadcasted_iota(jnp.int32, (Q_BLOCK, nwin), 0)
    wvalid = jnp.where(wpos <= tq_w, jnp.where(wpos > tq_w - WINDOW, jnp.where(wpos >= 0, 1.0, 0.0), 0.0), 0.0)
    wmask = wvalid > 0.5
    rounds = top - 3
    per_head = -(-rounds // HEADS_PER_KV)
    for r in range(HEADS_PER_KV):
        sm = jnp.where(wmask, s_w[rows[r]], NEG)
        pw = jnp.exp2(sm - jnp.max(sm, axis=-1, keepdims=True)).astype(BF16)
        res_w = jnp.dot(pw, vwin, preferred_element_type=F32)
        ow_ref[rows[r], :] = res_w[:, :HEAD_DIM] / res_w[:, HEAD_DIM:]
        for _ in range(min(per_head, rounds - r * per_head)):
            imp, sel_t = topk_round(imp, sel_t)
    sel_t = jnp.where(blk <= cur, sel_t, 0.0)
    sel = sel_t.T.astype(BF16)

    n_chunks = (t0 + Q_BLOCK + SEL_CHUNK - 1) // SEL_CHUNK
    diff = (lax.broadcasted_iota(jnp.int32, (nsel, SEL_SUB), 0)
            - lax.broadcasted_iota(jnp.int32, (nsel, SEL_SUB), 1) // SEL_BLOCK)
    lane_k = lax.broadcasted_iota(jnp.int32, (Q_BLOCK, SEL_SUB), 1)
    tq_s = t0 + lax.broadcasted_iota(jnp.int32, (Q_BLOCK, SEL_SUB), 0)
    lane_tiles = SEL_CHUNK // HEAD_DIM
    m_ref[...] = jnp.full(m_ref.shape, NEG, F32)

    def pass1(c, carry):
        best = [m_ref[rows[r], :] for r in range(HEADS_PER_KV)]
        for u in range(SEL_CHUNK // SEL_SUB):
            k0 = pl.multiple_of(c * SEL_CHUNK + u * SEL_SUB, SEL_SUB)
            s = _dot_nt(q4, ks_ref[pl.ds(k0, SEL_SUB), :])
            expand = jnp.where(diff == k0 // SEL_BLOCK, 1.0, 0.0).astype(BF16)
            picked = jnp.dot(sel, expand, preferred_element_type=F32)
            picked = jnp.where(k0 + lane_k <= tq_s, picked, 0.0)
            mask = picked > 0.5
            for r in range(HEADS_PER_KV):
                sm = jnp.where(mask, s[rows[r]], NEG)
                sc_ref[c, rows[r], u * SEL_SUB:(u + 1) * SEL_SUB] = sm
                best[r] = jnp.maximum(best[r], jnp.maximum(sm[:, :HEAD_DIM], sm[:, HEAD_DIM:]))
        for r in range(HEADS_PER_KV):
            m_ref[rows[r], :] = best[r]
        return carry

    lax.fori_loop(0, n_chunks, pass1, 0)
    m_ref[...] = jnp.broadcast_to(jnp.max(m_ref[...], axis=-1, keepdims=True), m_ref.shape)

    acc_ref[...] = jnp.zeros(acc_ref.shape, F32)

    def pass2(c, carry):
        k0 = pl.multiple_of(c * SEL_CHUNK, SEL_CHUNK)
        p = jnp.exp2(sc_ref[c] - jnp.concatenate([m_ref[...]] * lane_tiles, axis=1)).astype(BF16)
        acc_ref[...] += jnp.dot(p, vs_ref[pl.ds(k0, SEL_CHUNK), :], preferred_element_type=F32)
        return carry

    lax.fori_loop(0, n_chunks, pass2, 0)

    gate = jax.nn.sigmoid(gate_ref[0])
    for r in range(HEADS_PER_KV):
        o_win = ow_ref[rows[r], :]
        res_s = acc_ref[rows[r], :]
        o_sel = res_s[:, :HEAD_DIM] / res_s[:, HEAD_DIM:]
        g_c = gate[:, r:r + 1]
        g_s = gate[:, HEADS_PER_KV + r:HEADS_PER_KV + r + 1]
        g_w = gate[:, 2 * HEADS_PER_KV + r:2 * HEADS_PER_KV + r + 1]
        o_ref[:, r * HEAD_DIM:(r + 1) * HEAD_DIM] = g_c * o_cmp[r] + g_s * o_sel + g_w * o_win


def _overlap_t(seq):
    ncp = seq // CMP_STRIDE
    nsel = seq // SEL_BLOCK
    c_start = np.arange(ncp) * CMP_STRIDE
    j_start = np.arange(nsel) * SEL_BLOCK
    ov = ((c_start[None, :] < j_start[:, None] + SEL_BLOCK) & (c_start[None, :] + CMP_LEN > j_start[:, None])
          & (np.arange(ncp)[None, :] < ncp - 1))
    return jnp.asarray(ov.astype(np.float32), BF16)


def nsa_attention(q_r, gates_g, kc, vc_aug, ks_r, vs_aug, kw_pad, vw_pad, *, name):
    seq = q_r.shape[0]
    nq = seq // Q_BLOCK
    ncp = seq // CMP_STRIDE
    grp = HEADS_PER_KV * HEAD_DIM
    return pl.pallas_call(
        functools.partial(_nsa_body, seq=seq),
        grid=(N_KV, nq),
        in_specs=[pl.BlockSpec((Q_BLOCK, grp), lambda g, i: (i, g)),
                  pl.BlockSpec((1, Q_BLOCK, HEAD_DIM), lambda g, i: (g, i, 0)),
                  pl.BlockSpec((1, ncp, HEAD_DIM), lambda g, i: (g, 0, 0)),
                  pl.BlockSpec((1, ncp, 2 * HEAD_DIM), lambda g, i: (g, 0, 0)),
                  pl.BlockSpec((seq, HEAD_DIM), lambda g, i: (0, g)),
                  pl.BlockSpec((seq, 2 * HEAD_DIM), lambda g, i: (0, g)),
                  pl.BlockSpec((seq + WINDOW, HEAD_DIM), lambda g, i: (0, g)),
                  pl.BlockSpec((seq + WINDOW, 2 * HEAD_DIM), lambda g, i: (0, g)),
                  pl.BlockSpec((seq // SEL_BLOCK, ncp), lambda g, i: (0, 0))],
        out_specs=pl.BlockSpec((Q_BLOCK, grp), lambda g, i: (i, g)),
        out_shape=jax.ShapeDtypeStruct((seq, ATTN_WIDTH), F32),
        scratch_shapes=[pltpu.VMEM((seq // SEL_CHUNK, HEADS_PER_KV * Q_BLOCK, SEL_CHUNK), F32),
                        pltpu.VMEM((HEADS_PER_KV * Q_BLOCK, HEAD_DIM), F32),
                        pltpu.VMEM((HEADS_PER_KV * Q_BLOCK, 2 * HEAD_DIM), F32),
                        pltpu.VMEM((HEADS_PER_KV * Q_BLOCK, HEAD_DIM), F32)],
        compiler_params=pltpu.CompilerParams(dimension_semantics=("parallel", "arbitrary"),
                                             vmem_limit_bytes=VMEM_LIMIT_BYTES),
        name=name,
    )(q_r, gates_g, kc, vc_aug, ks_r, vs_aug, kw_pad, vw_pad, _overlap_t(seq))


def nsa_mixer(proj, gate_logits, cmp_pos, cmp_w1, cmp_w2, cos, sin_signed, *, tag):
    seq = proj.shape[0]
    q_r, ks_r, vs_aug, kw_pad, vw_pad = nsa_prep(proj, cos, sin_signed, name="nsa_prep" + tag)
    kc = nsa_compress(proj, KC_OFF, cos, sin_signed, cmp_pos[0], cmp_w1[0], cmp_w2[0],
                      rope=True, aug=False, name="nsa_cmp_k" + tag)
    vc_aug = nsa_compress(proj, VC_OFF, cos, sin_signed, cmp_pos[1], cmp_w1[1], cmp_w2[1],
                          rope=False, aug=True, name="nsa_cmp_v" + tag)
    gl = gate_logits.reshape(seq, 3, N_KV, HEADS_PER_KV).transpose(2, 0, 1, 3).reshape(N_KV, seq, 3 * HEADS_PER_KV)
    gl = jnp.pad(gl, ((0, 0), (0, 0), (0, HEAD_DIM - 3 * HEADS_PER_KV)))
    return nsa_attention(q_r, gl, kc, vc_aug, ks_r, vs_aug, kw_pad, vw_pad, name="nsa_attention" + tag)


LANES = 128
SUBLANES = 8


def _exact_dot(x, m):
    return sum(jnp.dot(part, m, preferred_element_type=F32) for part in _split3(x))


def _exact_dot_left(m, x):
    return sum(jnp.dot(m, part, preferred_element_type=F32) for part in _split3(x))


def _ssd_body(z_ref, xbc_ref, dt_ref, convw_ref, convb_ref, dtb_ref, alog_ref, dskip_ref, normw_ref,
              tril_ref, hexp_ref, o_ref, xcat_ref, act_ref, acsx_ref, dtx_ref, y_ref, state_ref):
    c = pl.program_id(0)
    L = SSD_CHUNK
    half = lax.broadcasted_iota(jnp.int32, (L, LANES), 1) < SSM_HEAD_DIM

    @pl.when(c == 0)
    def _():
        xcat_ref[0:SUBLANES, :] = jnp.zeros((SUBLANES, CONV_DIM), F32)
        state_ref[...] = jnp.zeros(state_ref.shape, F32)

    xcat_ref[SUBLANES:SUBLANES + L, :] = xbc_ref[...]
    slab = 512
    for j in range(CONV_DIM // slab):
        cs = slice(j * slab, (j + 1) * slab)
        conv = convb_ref[:, cs] + xcat_ref[pl.ds(SUBLANES - 3, L), cs] * convw_ref[0:1, cs]
        for k in range(1, CONV_WIDTH):
            conv = conv + xcat_ref[pl.ds(SUBLANES - 3 + k, L), cs] * convw_ref[k:k + 1, cs]
        act_ref[:, cs] = conv * jax.nn.sigmoid(conv)
    xcat_ref[0:SUBLANES, :] = xcat_ref[L:L + SUBLANES, :]

    raw = dt_ref[...] + dtb_ref[...]
    dtv = jnp.maximum(raw, 0.0) + jnp.log1p(jnp.exp(-jnp.abs(raw)))
    a = dtv * (-jnp.exp(alog_ref[...]) * LOG2_E)
    acs = _exact_dot_left(tril_ref[...], a)
    acs_t = acs.T
    hexp = hexp_ref[...]
    acsx_ref[...] = _exact_dot(acs, hexp)
    dtx_ref[...] = _exact_dot(dtv, hexp)

    tri = lax.broadcasted_iota(jnp.int32, (L, L), 0) >= lax.broadcasted_iota(jnp.int32, (L, L), 1)
    bcol = SSM_WIDTH
    ccol = SSM_WIDTH + SSM_GROUPS * SSM_STATE
    for g in range(SSM_GROUPS):
        bg = act_ref[:, bcol + g * SSM_STATE:bcol + (g + 1) * SSM_STATE]
        cg = act_ref[:, ccol + g * SSM_STATE:ccol + (g + 1) * SSM_STATE].astype(BF16)
        cb = _dot_nt(cg, bg.astype(BF16))
        bg_t = bg.T.astype(BF16)
        for jj in range(2):
            j = 2 * g + jj
            h0, h1 = 2 * j, 2 * j + 1
            ts = slice(j * LANES, (j + 1) * LANES)
            xs = act_ref[:, ts]
            col0 = acsx_ref[:, h0 * LANES:(h0 + 1) * LANES]
            col1 = acsx_ref[:, h1 * LANES:(h1 + 1) * LANES]
            acs_pair = jnp.where(half, col0, col1)
            dt_pair = jnp.where(half, dtx_ref[:, h0 * LANES:(h0 + 1) * LANES], dtx_ref[:, h1 * LANES:(h1 + 1) * LANES])
            xdt = xs * dt_pair
            xdt16 = xdt.astype(BF16)
            zero16 = jnp.zeros_like(xdt16)
            seg0 = jnp.concatenate([col0, col0], axis=1) - acs_t[h0:h0 + 1, :]
            seg1 = jnp.concatenate([col1, col1], axis=1) - acs_t[h1:h1 + 1, :]
            g0 = (cb * jnp.exp2(jnp.where(tri, seg0, NEG))).astype(BF16)
            g1 = (cb * jnp.exp2(jnp.where(tri, seg1, NEG))).astype(BF16)
            y = (jnp.dot(g0, jnp.where(half, xdt16, zero16), preferred_element_type=F32)
                 + jnp.dot(g1, jnp.where(half, zero16, xdt16), preferred_element_type=F32))
            st = state_ref[j]
            y = y + jnp.dot(cg, st.astype(BF16), preferred_element_type=F32) * jnp.exp2(acs_pair)
            last = acs_pair[L - 1:L, :]
            xw = (xdt * jnp.exp2(last - acs_pair)).astype(BF16)
            state_ref[j] = jnp.exp2(last) * st + jnp.dot(bg_t, xw, preferred_element_type=F32)
            y_ref[:, ts] = y + dskip_ref[:, ts] * xs

    gw = SSM_WIDTH // SSM_GROUPS
    for g in range(SSM_GROUPS):
        gs = slice(g * gw, (g + 1) * gw)
        zz = z_ref[:, gs]
        yz = y_ref[:, gs] * (zz * jax.nn.sigmoid(zz))
        ms = jnp.mean(yz * yz, axis=-1, keepdims=True)
        o_ref[:, gs] = (yz * lax.rsqrt(ms + EPS) * normw_ref[:, gs]).astype(o_ref.dtype)


def ssd_mixer(proj, z_col, xbc_col, small, conv_w, conv_b, dt_bias, a_log, d_skip, norm_w, *, name):
    seq = proj.shape[0]
    L = SSD_CHUNK
    pad = lambda v: jnp.pad(v.astype(F32), (0, LANES - v.shape[0])).reshape(1, LANES)
    tril = jnp.asarray(np.tril(np.ones((L, L), np.float32)), BF16)
    hexp = jnp.asarray((np.arange(LANES)[:, None] == np.arange(SSM_HEADS * LANES)[None, :] // LANES)
                       .astype(np.float32), BF16)
    dskip_x = jnp.repeat(d_skip.astype(F32), SSM_HEAD_DIM).reshape(1, SSM_WIDTH)
    full = lambda shape: pl.BlockSpec(shape, lambda c: (0,) * len(shape))
    return pl.pallas_call(
        _ssd_body,
        grid=(seq // L,),
        in_specs=[pl.BlockSpec((L, SSM_WIDTH), lambda c, o=z_col // SSM_WIDTH: (c, o)),
                  pl.BlockSpec((L, CONV_DIM), lambda c, o=xbc_col // CONV_DIM: (c, o)),
                  pl.BlockSpec((L, LANES), lambda c: (c, 0)),
                  full((CONV_WIDTH, CONV_DIM)), full((1, CONV_DIM)), full((1, LANES)), full((1, LANES)),
                  full((1, SSM_WIDTH)), full((1, SSM_WIDTH)), full((L, L)), full((LANES, SSM_HEADS * LANES))],
        out_specs=pl.BlockSpec((L, SSM_WIDTH), lambda c: (c, 0)),
        out_shape=jax.ShapeDtypeStruct((seq, SSM_WIDTH), BF16),
        scratch_shapes=[pltpu.VMEM((L + SUBLANES, CONV_DIM), F32),
                        pltpu.VMEM((L, CONV_DIM), F32),
                        pltpu.VMEM((L, SSM_HEADS * LANES), F32),
                        pltpu.VMEM((L, SSM_HEADS * LANES), F32),
                        pltpu.VMEM((L, SSM_WIDTH), F32),
                        pltpu.VMEM((SSM_HEADS // 2, SSM_STATE, LANES), F32)],
        compiler_params=pltpu.CompilerParams(dimension_semantics=("arbitrary",),
                                             vmem_limit_bytes=VMEM_LIMIT_BYTES),
        name=name,
    )(proj, proj, small, conv_w, conv_b.reshape(1, CONV_DIM), pad(dt_bias), pad(a_log), dskip_x,
      norm_w.reshape(1, SSM_WIDTH).astype(F32), tril, hexp)


def _rope_tables(seq):
    inv = 1.0 / (ROPE_THETA ** (jnp.arange(0, HEAD_DIM, 2, dtype=F32) / HEAD_DIM))
    ang = jnp.arange(seq, dtype=F32)[:, None] * inv[None, :]
    ang = jnp.concatenate([ang, ang], axis=-1)
    return jnp.cos(ang), jnp.sin(ang)


def kernel(x, norm_mix, w_in, cmp_pos, cmp_w1, cmp_w2, attn_norm, conv_w, conv_b, dt_bias, a_log,
           d_skip, ssm_norm, w_out, norm_mlp, w_up, w_down, norm_final):
    batch, seq, _ = x.shape
    assert batch == 1
    cos, sin = _rope_tables(seq)
    half_sign = jnp.concatenate([-jnp.ones((HEAD_DIM // 2,), F32), jnp.ones((HEAD_DIM // 2,), F32)])
    sin_signed = sin * half_sign
    h = x.reshape(seq, D_MODEL)
    w_big = reorder_w_in(w_in)
    w_down16 = w_down.astype(BF16)
    w_small = jnp.concatenate(
        [w_in[:, :, W_DT_OFF:], w_in[:, :, W_GATE_OFF:W_Z_OFF],
         jnp.zeros((DEPTH, D_MODEL, SMALL_COLS - SSM_HEADS - GATE_COLS), F32)], axis=2).astype(BF16)
    for l in range(DEPTH):
        xn = rmsnorm(h, norm_mix[l], out_dtype=BF16, name=f"norm_mix{l}")
        proj = matmul(xn, w_big, l, name=f"in_proj{l}")
        small = matmul(xn, w_small, l, name=f"in_proj_small{l}")
        attn = nsa_mixer(proj, small[:, SSM_HEADS:SSM_HEADS + GATE_COLS], cmp_pos[l], cmp_w1[l], cmp_w2[l],
                         cos, sin_signed, tag=str(l))
        ssm = ssd_mixer(proj, Z_COL, XBC_COL, small, conv_w[l], conv_b[l], dt_bias[l], a_log[l], d_skip[l],
                        ssm_norm[l], name=f"ssd{l}")
        mix = norm_concat(attn, attn_norm[l], ssm, name=f"norm_concat{l}")
        h = matmul(mix, w_out, l, residual=h, bn=512, name=f"out_proj{l}")
        hn = rmsnorm(h, norm_mlp[l], out_dtype=BF16, name=f"norm_mlp{l}")
        u = matmul(hn, w_up, l, act="relu2", out_dtype=BF16, bn=512, name=f"mlp_up{l}")
        h = matmul(u, w_down16, l, residual=h, bk=2048, name=f"mlp_down{l}")
    out = rmsnorm(h, norm_final, out_dtype=F32, name="norm_final")
    return out.reshape(batch, seq, D_MODEL)
```

```python
import functools

import jax
import jax.numpy as jnp
import numpy as np
from jax import lax
from jax.experimental import pallas as pl
from jax.experimental.pallas import tpu as pltpu

F32 = jnp.float32
BF16 = jnp.bfloat16

D_MODEL = 4096
DEPTH = 2
ATTN_WIDTH = 2048
SSM_WIDTH = 2048
HEAD_DIM = 128
N_HEADS = 16
N_KV = 4
HEADS_PER_KV = 4
KV_WIDTH = 512
ROPE_THETA = 10000.0
CMP_LEN = 32
CMP_STRIDE = 16
SEL_BLOCK = 64
SEL_TOP = 16
WINDOW = 512
Q_BLOCK = 128
BIG = 1e9
SSM_HEAD_DIM = 64
SSM_HEADS = 32
SSM_GROUPS = 8
SSM_STATE = 128
CONV_WIDTH = 4
SSD_CHUNK = 256
CONV_DIM = SSM_WIDTH + 2 * SSM_GROUPS * SSM_STATE
D_FF = 4 * D_MODEL
EPS = 1e-5
NEG = -1e30

VMEM_LIMIT_BYTES = 52 * 1024 * 1024

GATE_COLS = 3 * N_HEADS
W_KV_OFF = ATTN_WIDTH
W_GATE_OFF = W_KV_OFF + 6 * KV_WIDTH
W_Z_OFF = W_GATE_OFF + GATE_COLS
W_XBC_OFF = W_Z_OFF + SSM_WIDTH
W_DT_OFF = W_XBC_OFF + CONV_DIM
Z_COL = ATTN_WIDTH
XBC_COL = Z_COL + SSM_WIDTH
KC_OFF = XBC_COL + CONV_DIM
VC_OFF, KS_OFF, VS_OFF, KW_OFF, VW_OFF = (KC_OFF + i * KV_WIDTH for i in range(1, 6))
SMALL_COLS = 128


def _matmul_body(*refs, nk, act, has_res):
    a_ref, b_ref = refs[0], refs[1]
    r_ref = refs[2] if has_res else None
    o_ref = refs[3] if has_res else refs[2]
    part = jnp.dot(a_ref[...], b_ref[...].astype(BF16), preferred_element_type=F32)

    def finish(acc):
        if act == "relu2":
            acc = jnp.square(jnp.maximum(acc, 0.0))
        if has_res:
            acc = acc + r_ref[...]
        o_ref[...] = acc.astype(o_ref.dtype)

    if nk == 1:
        finish(part)
    else:
        acc_ref = refs[-1]
        k = pl.program_id(2)

        @pl.when(k == 0)
        def _():
            acc_ref[...] = part

        @pl.when(k > 0)
        def _():
            acc_ref[...] += part

        @pl.when(k == nk - 1)
        def _():
            finish(acc_ref[...])


def matmul(a, b, layer, *, residual=None, act=None, out_dtype=F32, bm=1024, bn=1024, bk=None, name):
    m, kdim = a.shape
    _, _, n = b.shape
    bk = kdim if bk is None else bk
    bn = min(bn, n)
    assert m % bm == 0 and n % bn == 0 and kdim % bk == 0
    nk = kdim // bk
    in_specs = [pl.BlockSpec((bm, bk), lambda i, j, k: (i, k)),
                pl.BlockSpec((None, bk, bn), lambda i, j, k: (layer, k, j))]
    args = [a, b]
    if residual is not None:
        in_specs.append(pl.BlockSpec((bm, bn), lambda i, j, k: (i, j)))
        args.append(residual)
    return pl.pallas_call(
        functools.partial(_matmul_body, nk=nk, act=act, has_res=residual is not None),
        grid=(m // bm, n // bn, nk),
        in_specs=in_specs,
        out_specs=pl.BlockSpec((bm, bn), lambda i, j, k: (i, j)),
        out_shape=jax.ShapeDtypeStruct((m, n), out_dtype),
        scratch_shapes=[pltpu.VMEM((bm, bn), F32)] if nk > 1 else [],
        compiler_params=pltpu.CompilerParams(
            dimension_semantics=("parallel", "parallel", "arbitrary"),
            vmem_limit_bytes=VMEM_LIMIT_BYTES),
        name=name,
    )(*args)


def _rmsnorm_body(x_ref, w_ref, o_ref):
    x = x_ref[...]
    y = x * lax.rsqrt(jnp.mean(x * x, axis=-1, keepdims=True) + EPS)
    o_ref[...] = (y * w_ref[...]).astype(o_ref.dtype)


def rmsnorm(x, w, *, out_dtype, rows=256, name):
    m, d = x.shape
    return pl.pallas_call(
        _rmsnorm_body,
        grid=(m // rows,),
        in_specs=[pl.BlockSpec((rows, d), lambda i: (i, 0)),
                  pl.BlockSpec((1, d), lambda i: (0, 0))],
        out_specs=pl.BlockSpec((rows, d), lambda i: (i, 0)),
        out_shape=jax.ShapeDtypeStruct((m, d), out_dtype),
        compiler_params=pltpu.CompilerParams(dimension_semantics=("parallel",)),
        name=name,
    )(x, w.reshape(1, d))


def _norm_concat_body(a_ref, w_ref, s_ref, o_ref):
    x = a_ref[...]
    y = x * lax.rsqrt(jnp.mean(x * x, axis=-1, keepdims=True) + EPS)
    o_ref[:, :ATTN_WIDTH] = (y * w_ref[...]).astype(o_ref.dtype)
    o_ref[:, ATTN_WIDTH:] = s_ref[...]


def norm_concat(attn, w, ssm, *, rows=256, name):
    m = attn.shape[0]
    return pl.pallas_call(
        _norm_concat_body,
        grid=(m // rows,),
        in_specs=[pl.BlockSpec((rows, ATTN_WIDTH), lambda i: (i, 0)),
                  pl.BlockSpec((1, ATTN_WIDTH), lambda i: (0, 0)),
                  pl.BlockSpec((rows, SSM_WIDTH), lambda i: (i, 0))],
        out_specs=pl.BlockSpec((rows, D_MODEL), lambda i: (i, 0)),
        out_shape=jax.ShapeDtypeStruct((m, D_MODEL), BF16),
        compiler_params=pltpu.CompilerParams(dimension_semantics=("parallel",)),
        name=name,
    )(attn, w.reshape(1, ATTN_WIDTH), ssm)


SEL_CHUNK = 1024
SEL_SUB = 256
LOG2_E = 1.4426950408889634


def _split3(x):
    hi = x.astype(BF16)
    r1 = x - hi.astype(F32)
    mid = r1.astype(BF16)
    lo = (r1 - mid.astype(F32)).astype(BF16)
    return hi, mid, lo


def _dot_nt(a, b):
    return lax.dot_general(a, b, (((1,), (1,)), ((), ())), preferred_element_type=F32)


def _prep_body(q_ref, ks_ref, vs_ref, kw_ref, vw_ref, cos_ref, sin_ref, qo, kso, vso, kwo, vwo):
    step = pl.program_id(0)

    @pl.when(step == 0)
    def _():
        kwo[...] = jnp.zeros(kwo.shape, BF16)
        vwo[...] = jnp.zeros(vwo.shape, BF16)

    @pl.when(step > 0)
    def _():
        _prep_rows(q_ref, ks_ref, vs_ref, kw_ref, vw_ref, cos_ref, sin_ref, qo, kso, vso, kwo, vwo)


def _prep_rows(q_ref, ks_ref, vs_ref, kw_ref, vw_ref, cos_ref, sin_ref, qo, kso, vso, kwo, vwo):
    cos = cos_ref[...]
    sin = sin_ref[...]
    scale = HEAD_DIM ** -0.5 * LOG2_E

    def rope(t):
        return t * cos + pltpu.roll(t, HEAD_DIM // 2, 1) * sin

    for h in range(N_HEADS):
        sl = slice(h * HEAD_DIM, (h + 1) * HEAD_DIM)
        qo[:, sl] = (rope(q_ref[:, sl]) * scale).astype(BF16)
    ones = jnp.ones((q_ref.shape[0], HEAD_DIM), BF16)
    for g in range(N_KV):
        sl = slice(g * HEAD_DIM, (g + 1) * HEAD_DIM)
        kso[:, sl] = rope(ks_ref[:, sl]).astype(BF16)
        kwo[:, sl] = rope(kw_ref[:, sl]).astype(BF16)
        vso[:, 2 * g * HEAD_DIM:(2 * g + 1) * HEAD_DIM] = vs_ref[:, sl].astype(BF16)
        vso[:, (2 * g + 1) * HEAD_DIM:(2 * g + 2) * HEAD_DIM] = ones
        vwo[:, 2 * g * HEAD_DIM:(2 * g + 1) * HEAD_DIM] = vw_ref[:, sl].astype(BF16)
        vwo[:, (2 * g + 1) * HEAD_DIM:(2 * g + 2) * HEAD_DIM] = ones


def nsa_prep(proj, cos, sin_signed, *, name):
    seq = proj.shape[0]
    rows = WINDOW
    prev = lambda i: jnp.maximum(i - 1, 0)
    kvb = lambda off: pl.BlockSpec((rows, KV_WIDTH), lambda i, o=off // KV_WIDTH: (prev(i), o))
    return pl.pallas_call(
        _prep_body,
        grid=(seq // rows + 1,),
        in_specs=[pl.BlockSpec((rows, ATTN_WIDTH), lambda i: (prev(i), 0)),
                  kvb(KS_OFF), kvb(VS_OFF), kvb(KW_OFF), kvb(VW_OFF),
                  pl.BlockSpec((rows, HEAD_DIM), lambda i: (prev(i), 0)),
                  pl.BlockSpec((rows, HEAD_DIM), lambda i: (prev(i), 0))],
        out_specs=[pl.BlockSpec((rows, ATTN_WIDTH), lambda i: (prev(i), 0)),
                   pl.BlockSpec((rows, KV_WIDTH), lambda i: (prev(i), 0)),
                   pl.BlockSpec((rows, 2 * KV_WIDTH), lambda i: (prev(i), 0)),
                   pl.BlockSpec((rows, KV_WIDTH), lambda i: (i, 0)),
                   pl.BlockSpec((rows, 2 * KV_WIDTH), lambda i: (i, 0))],
        out_shape=[jax.ShapeDtypeStruct((seq, ATTN_WIDTH), BF16),
                   jax.ShapeDtypeStruct((seq, KV_WIDTH), BF16),
                   jax.ShapeDtypeStruct((seq, 2 * KV_WIDTH), BF16),
                   jax.ShapeDtypeStruct((seq + WINDOW, KV_WIDTH), BF16),
                   jax.ShapeDtypeStruct((seq + WINDOW, 2 * KV_WIDTH), BF16)],
        compiler_params=pltpu.CompilerParams(dimension_semantics=("arbitrary",),
                                             vmem_limit_bytes=VMEM_LIMIT_BYTES),
        name=name,
    )(proj, proj, proj, proj, proj, cos, sin_signed)


def _compress_body(t_ref, cos_ref, sin_ref, pos_ref, w1_ref, w2_ref, o_ref, tr_ref, *, rope, aug):
    seq = t_ref.shape[0]
    ncp = seq // CMP_STRIDE
    t = t_ref[...]
    if rope:
        t = t * cos_ref[...] + pltpu.roll(t, HEAD_DIM // 2, 1) * sin_ref[...]
    tr_ref[...] = t
    acc_a = jnp.zeros((ncp, HEAD_DIM), F32)
    acc_b = jnp.zeros((ncp, HEAD_DIM), F32)
    for p in range(CMP_STRIDE):
        tp = tr_ref[pl.ds(p, ncp, stride=CMP_STRIDE), :]
        wa = w1_ref[p * HEAD_DIM:(p + 1) * HEAD_DIM, :].astype(BF16)
        wb = w1_ref[(CMP_STRIDE + p) * HEAD_DIM:(CMP_STRIDE + p + 1) * HEAD_DIM, :].astype(BF16)
        acc_a += jnp.dot((tp + pos_ref[p:p + 1, :]).astype(BF16), wa, preferred_element_type=F32)
        acc_b += jnp.dot((tp + pos_ref[CMP_STRIDE + p:CMP_STRIDE + p + 1, :]).astype(BF16), wb,
                         preferred_element_type=F32)
    pre = acc_a + pltpu.roll(acc_b, ncp - 1, 0)
    hid = pre * jax.nn.sigmoid(pre)
    out = jnp.dot(hid.astype(BF16), w2_ref[...].astype(BF16), preferred_element_type=F32)
    o_ref[0, :, :HEAD_DIM] = out.astype(BF16)
    if aug:
        o_ref[0, :, HEAD_DIM:] = jnp.ones((ncp, HEAD_DIM), BF16)


def nsa_compress(proj, col_off, cos, sin_signed, pos, w1, w2, *, rope, aug, name):
    seq = proj.shape[0]
    ncp = seq // CMP_STRIDE
    width = 2 * HEAD_DIM if aug else HEAD_DIM
    return pl.pallas_call(
        functools.partial(_compress_body, rope=rope, aug=aug),
        grid=(N_KV,),
        in_specs=[pl.BlockSpec((seq, HEAD_DIM), lambda g, o=col_off // HEAD_DIM: (0, o + g)),
                  pl.BlockSpec((seq, HEAD_DIM), lambda g: (0, 0)),
                  pl.BlockSpec((seq, HEAD_DIM), lambda g: (0, 0)),
                  pl.BlockSpec((CMP_LEN, HEAD_DIM), lambda g: (0, 0)),
                  pl.BlockSpec((CMP_LEN * HEAD_DIM, HEAD_DIM), lambda g: (0, 0)),
                  pl.BlockSpec((HEAD_DIM, HEAD_DIM), lambda g: (0, 0))],
        out_specs=pl.BlockSpec((1, ncp, width), lambda g: (g, 0, 0)),
        out_shape=jax.ShapeDtypeStruct((N_KV, ncp, width), BF16),
        scratch_shapes=[pltpu.VMEM((seq, HEAD_DIM), F32)],
        compiler_params=pltpu.CompilerParams(dimension_semantics=("parallel",),
                                             vmem_limit_bytes=VMEM_LIMIT_BYTES),
        name=name,
    )(proj, cos, sin_signed, pos, w1, w2)


def _nsa_body(q_ref, gate_ref, kc_ref, vc_ref, ks_ref, vs_ref, kw_ref, vw_ref, ovt_ref, o_ref,
              sc_ref, m_ref, acc_ref, ow_ref, *, seq):
    i = pl.program_id(1)
    t0 = i * Q_BLOCK
    ncp = seq // CMP_STRIDE
    nsel = seq // SEL_BLOCK
    top = min(SEL_TOP, nsel)
    nwin = WINDOW + Q_BLOCK
    rows = [slice(r * Q_BLOCK, (r + 1) * Q_BLOCK) for r in range(HEADS_PER_KV)]

    q4 = jnp.concatenate([q_ref[:, r * HEAD_DIM:(r + 1) * HEAD_DIM] for r in range(HEADS_PER_KV)], axis=0)

    s_c = _dot_nt(q4, kc_ref[0])
    c_end = lax.broadcasted_iota(jnp.int32, (Q_BLOCK, ncp), 1) * CMP_STRIDE + (CMP_LEN - 1)
    tq_c = t0 + lax.broadcasted_iota(jnp.int32, (Q_BLOCK, ncp), 0)
    cmask = c_end <= tq_c
    vc = vc_ref[0]
    psum = jnp.zeros((Q_BLOCK, ncp), F32)
    o_cmp = []
    for r in range(HEADS_PER_KV):
        s = jnp.where(cmask, s_c[rows[r]], NEG)
        m = jnp.max(s, axis=-1, keepdims=True)
        p = jnp.where(cmask, jnp.exp2(s - m), 0.0)
        pn = p / jnp.maximum(jnp.sum(p, axis=-1, keepdims=True), 1e-30)
        psum = psum + pn
        o_cmp.append(jnp.dot(pn.astype(BF16), vc[:, :HEAD_DIM], preferred_element_type=F32))

    ovt = ovt_ref[...]
    imp = sum(_dot_nt(ovt, part) for part in _split3(psum))
    blk = lax.broadcasted_iota(jnp.int32, (nsel, Q_BLOCK), 0)
    cur = (t0 + lax.broadcasted_iota(jnp.int32, (nsel, Q_BLOCK), 1)) // SEL_BLOCK
    blkf = blk.astype(F32)
    forced = jnp.where(blk == 0, 1.0, jnp.where(blk == cur, 1.0, jnp.where(blk == cur - 1, 1.0, 0.0)))
    imp = jnp.where(forced > 0.5, -jnp.inf, jnp.where(blk > cur, -BIG, imp))
    sel_t = forced

    def topk_round(imp, sel_t):
        mx = jnp.max(imp, axis=0, keepdims=True)
        first = jnp.min(jnp.where(imp == mx, blkf, float(nsel)), axis=0, keepdims=True)
        hit = blkf == first
        return jnp.where(hit, -jnp.inf, imp), jnp.where(hit, 1.0, sel_t)

    w0 = pl.multiple_of(t0, Q_BLOCK)
    s_w = _dot_nt(q4, kw_ref[pl.ds(w0, nwin), :])
    vwin = vw_ref[pl.ds(w0, nwin), :]
    wpos = t0 - WINDOW + lax.broadcasted_iota(jnp.int32, (Q_BLOCK, nwin), 1)
    tq_w = t0 + lax.broadcasted_iota(jnp.int32, (Q_BLOCK, nwin), 0)
    wvalid = jnp.where(wpos <= tq_w, jnp.where(wpos > tq_w - WINDOW, jnp.where(wpos >= 0, 1.0, 0.0), 0.0), 0.0)
    wmask = wvalid > 0.5
    rounds = top - 3
    per_head = -(-rounds // HEADS_PER_KV)
    for r in range(HEADS_PER_KV):
        sm = jnp.where(wmask, s_w[rows[r]], NEG)
        pw = jnp.exp2(sm - jnp.max(sm, axis=-1, keepdims=True)).astype(BF16)
        res_w = jnp.dot(pw, vwin, preferred_element_type=F32)
        ow_ref[rows[r], :] = res_w[:, :HEAD_DIM] / res_w[:, HEAD_DIM:]
        for _ in range(min(per_head, rounds - r * per_head)):
            imp, sel_t = topk_round(imp, sel_t)
    sel_t = jnp.where(blk <= cur, sel_t, 0.0)
    sel = sel_t.T.astype(BF16)

    n_chunks = (t0 + Q_BLOCK + SEL_CHUNK - 1) // SEL_CHUNK
    diff = (lax.broadcasted_iota(jnp.int32, (nsel, SEL_SUB), 0)
            - lax.broadcasted_iota(jnp.int32, (nsel, SEL_SUB), 1) // SEL_BLOCK)
    lane_k = lax.broadcasted_iota(jnp.int32, (Q_BLOCK, SEL_SUB), 1)
    tq_s = t0 + lax.broadcasted_iota(jnp.int32, (Q_BLOCK, SEL_SUB), 0)
    lane_tiles = SEL_CHUNK // HEAD_DIM
    m_ref[...] = jnp.full(m_ref.shape, NEG, F32)

    def pass1(c, carry):
        best = [m_ref[rows[r], :] for r in range(HEADS_PER_KV)]
        for u in range(SEL_CHUNK // SEL_SUB):
            k0 = pl.multiple_of(c * SEL_CHUNK + u * SEL_SUB, SEL_SUB)
            s = _dot_nt(q4, ks_ref[pl.ds(k0, SEL_SUB), :])
            expand = jnp.where(diff == k0 // SEL_BLOCK, 1.0, 0.0).astype(BF16)
            picked = jnp.dot(sel, expand, preferred_element_type=F32)
            picked = jnp.where(k0 + lane_k <= tq_s, picked, 0.0)
            mask = picked > 0.5
            for r in range(HEADS_PER_KV):
                sm = jnp.where(mask, s[rows[r]], NEG)
                sc_ref[c, rows[r], u * SEL_SUB:(u + 1) * SEL_SUB] = sm
                best[r] = jnp.maximum(best[r], jnp.maximum(sm[:, :HEAD_DIM], sm[:, HEAD_DIM:]))
        for r in range(HEADS_PER_KV):
            m_ref[rows[r], :] = best[r]
        return carry

    lax.fori_loop(0, n_chunks, pass1, 0)
    m_ref[...] = jnp.broadcast_to(jnp.max(m_ref[...], axis=-1, keepdims=True), m_ref.shape)

    acc_ref[...] = jnp.zeros(acc_ref.shape, F32)

    def pass2(c, carry):
        k0 = pl.multiple_of(c * SEL_CHUNK, SEL_CHUNK)
        p = jnp.exp2(sc_ref[c] - jnp.concatenate([m_ref[...]] * lane_tiles, axis=1)).astype(BF16)
        acc_ref[...] += jnp.dot(p, vs_ref[pl.ds(k0, SEL_CHUNK), :], preferred_element_type=F32)
        return carry

    lax.fori_loop(0, n_chunks, pass2, 0)

    gate = jax.nn.sigmoid(gate_ref[0])
    for r in range(HEADS_PER_KV):
        o_win = ow_ref[rows[r], :]
        res_s = acc_ref[rows[r], :]
        o_sel = res_s[:, :HEAD_DIM] / res_s[:, HEAD_DIM:]
        g_c = gate[:, r:r + 1]
        g_s = gate[:, HEADS_PER_KV + r:HEADS_PER_KV + r + 1]
        g_w = gate[:, 2 * HEADS_PER_KV + r:2 * HEADS_PER_KV + r + 1]
        o_ref[:, r * HEAD_DIM:(r + 1) * HEAD_DIM] = g_c * o_cmp[r] + g_s * o_sel + g_w * o_win


def _overlap_t(seq):
    ncp = seq // CMP_STRIDE
    nsel = seq // SEL_BLOCK
    c_start = np.arange(ncp) * CMP_STRIDE
    j_start = np.arange(nsel) * SEL_BLOCK
    ov = ((c_start[None, :] < j_start[:, None] + SEL_BLOCK) & (c_start[None, :] + CMP_LEN > j_start[:, None])
          & (np.arange(ncp)[None, :] < ncp - 1))
    return jnp.asarray(ov.astype(np.float32), BF16)


def nsa_attention(q_r, gates_g, kc, vc_aug, ks_r, vs_aug, kw_pad, vw_pad, *, name):
    seq = q_r.shape[0]
    nq = seq // Q_BLOCK
    ncp = seq // CMP_STRIDE
    grp = HEADS_PER_KV * HEAD_DIM
    return pl.pallas_call(
        functools.partial(_nsa_body, seq=seq),
        grid=(N_KV, nq),
        in_specs=[pl.BlockSpec((Q_BLOCK, grp), lambda g, i: (i, g)),
                  pl.BlockSpec((1, Q_BLOCK, HEAD_DIM), lambda g, i: (g, i, 0)),
                  pl.BlockSpec((1, ncp, HEAD_DIM), lambda g, i: (g, 0, 0)),
                  pl.BlockSpec((1, ncp, 2 * HEAD_DIM), lambda g, i: (g, 0, 0)),
                  pl.BlockSpec((seq, HEAD_DIM), lambda g, i: (0, g)),
                  pl.BlockSpec((seq, 2 * HEAD_DIM), lambda g, i: (0, g)),
                  pl.BlockSpec((seq + WINDOW, HEAD_DIM), lambda g, i: (0, g)),
                  pl.BlockSpec((seq + WINDOW, 2 * HEAD_DIM), lambda g, i: (0, g)),
                  pl.BlockSpec((seq // SEL_BLOCK, ncp), lambda g, i: (0, 0))],
        out_specs=pl.BlockSpec((Q_BLOCK, grp), lambda g, i: (i, g)),
        out_shape=jax.ShapeDtypeStruct((seq, ATTN_WIDTH), F32),
        scratch_shapes=[pltpu.VMEM((seq // SEL_CHUNK, HEADS_PER_KV * Q_BLOCK, SEL_CHUNK), F32),
                        pltpu.VMEM((HEADS_PER_KV * Q_BLOCK, HEAD_DIM), F32),
                        pltpu.VMEM((HEADS_PER_KV * Q_BLOCK, 2 * HEAD_DIM), F32),
                        pltpu.VMEM((HEADS_PER_KV * Q_BLOCK, HEAD_DIM), F32)],
        compiler_params=pltpu.CompilerParams(dimension_semantics=("parallel", "arbitrary"),
                                             vmem_limit_bytes=VMEM_LIMIT_BYTES),
        name=name,
    )(q_r, gates_g, kc, vc_aug, ks_r, vs_aug, kw_pad, vw_pad, _overlap_t(seq))


def nsa_mixer(proj, gate_logits, cmp_pos, cmp_w1, cmp_w2, cos, sin_signed, *, tag):
    seq = proj.shape[0]
    q_r, ks_r, vs_aug, kw_pad, vw_pad = nsa_prep(proj, cos, sin_signed, name="nsa_prep" + tag)
    kc = nsa_compress(proj, KC_OFF, cos, sin_signed, cmp_pos[0], cmp_w1[0], cmp_w2[0],
                      rope=True, aug=False, name="nsa_cmp_k" + tag)
    vc_aug = nsa_compress(proj, VC_OFF, cos, sin_signed, cmp_pos[1], cmp_w1[1], cmp_w2[1],
                          rope=False, aug=True, name="nsa_cmp_v" + tag)
    gl = gate_logits.reshape(seq, 3, N_KV, HEADS_PER_KV).transpose(2, 0, 1, 3).reshape(N_KV, seq, 3 * HEADS_PER_KV)
    gl = jnp.pad(gl, ((0, 0), (0, 0), (0, HEAD_DIM - 3 * HEADS_PER_KV)))
    return nsa_attention(q_r, gl, kc, vc_aug, ks_r, vs_aug, kw_pad, vw_pad, name="nsa_attention" + tag)


LANES = 128
SUBLANES = 8


def _exact_dot(x, m):
    return sum(jnp.dot(part, m, preferred_element_type=F32) for part in _split3(x))


def _exact_dot_left(m, x):
    return sum(jnp.dot(m, part, preferred_element_type=F32) for part in _split3(x))


def _ssd_body(z_ref, xbc_ref, dt_ref, convw_ref, convb_ref, dtb_ref, alog_ref, dskip_ref, normw_ref,
              tril_ref, hexp_ref, o_ref, xcat_ref, act_ref, acsx_ref, dtx_ref, y_ref, state_ref):
    c = pl.program_id(0)
    L = SSD_CHUNK
    half = lax.broadcasted_iota(jnp.int32, (L, LANES), 1) < SSM_HEAD_DIM

    @pl.when(c == 0)
    def _():
        xcat_ref[0:SUBLANES, :] = jnp.zeros((SUBLANES, CONV_DIM), F32)
        state_ref[...] = jnp.zeros(state_ref.shape, F32)

    xcat_ref[SUBLANES:SUBLANES + L, :] = xbc_ref[...]
    slab = 512
    for j in range(CONV_DIM // slab):
        cs = slice(j * slab, (j + 1) * slab)
        conv = convb_ref[:, cs] + xcat_ref[pl.ds(SUBLANES - 3, L), cs] * convw_ref[0:1, cs]
        for k in range(1, CONV_WIDTH):
            conv = conv + xcat_ref[pl.ds(SUBLANES - 3 + k, L), cs] * convw_ref[k:k + 1, cs]
        act_ref[:, cs] = conv * jax.nn.sigmoid(conv)
    xcat_ref[0:SUBLANES, :] = xcat_ref[L:L + SUBLANES, :]

    raw = dt_ref[...] + dtb_ref[...]
    dtv = jnp.maximum(raw, 0.0) + jnp.log1p(jnp.exp(-jnp.abs(raw)))
    a = dtv * (-jnp.exp(alog_ref[...]) * LOG2_E)
    acs = _exact_dot_left(tril_ref[...], a)
    acs_t = acs.T
    hexp = hexp_ref[...]
    acsx_ref[...] = _exact_dot(acs, hexp)
    dtx_ref[...] = _exact_dot(dtv, hexp)

    tri = lax.broadcasted_iota(jnp.int32, (L, L), 0) >= lax.broadcasted_iota(jnp.int32, (L, L), 1)
    bcol = SSM_WIDTH
    ccol = SSM_WIDTH + SSM_GROUPS * SSM_STATE
    for g in range(SSM_GROUPS):
        bg = act_ref[:, bcol + g * SSM_STATE:bcol + (g + 1) * SSM_STATE]
        cg = act_ref[:, ccol + g * SSM_STATE:ccol + (g + 1) * SSM_STATE].astype(BF16)
        cb = _dot_nt(cg, bg.astype(BF16))
        bg_t = bg.T.astype(BF16)
        for jj in range(2):
            j = 2 * g + jj
            h0, h1 = 2 * j, 2 * j + 1
            ts = slice(j * LANES, (j + 1) * LANES)
            xs = act_ref[:, ts]
            col0 = acsx_ref[:, h0 * LANES:(h0 + 1) * LANES]
            col1 = acsx_ref[:, h1 * LANES:(h1 + 1) * LANES]
            acs_pair = jnp.where(half, col0, col1)
            dt_pair = jnp.where(half, dtx_ref[:, h0 * LANES:(h0 + 1) * LANES], dtx_ref[:, h1 * LANES:(h1 + 1) * LANES])
            xdt = xs * dt_pair
            xdt16 = xdt.astype(BF16)
            zero16 = jnp.zeros_like(xdt16)
            seg0 = jnp.concatenate([col0, col0], axis=1) - acs_t[h0:h0 + 1, :]
            seg1 = jnp.concatenate([col1, col1], axis=1) - acs_t[h1:h1 + 1, :]
            g0 = (cb * jnp.exp2(jnp.where(tri, seg0, NEG))).astype(BF16)
            g1 = (cb * jnp.exp2(jnp.where(tri, seg1, NEG))).astype(BF16)
            y = (jnp.dot(g0, jnp.where(half, xdt16, zero16), preferred_element_type=F32)
                 + jnp.dot(g1, jnp.where(half, zero16, xdt16), preferred_element_type=F32))
            st = state_ref[j]
            y = y + jnp.dot(cg, st.astype(BF16), preferred_element_type=F32) * jnp.exp2(acs_pair)
            last = acs_pair[L - 1:L, :]
            xw = (xdt * jnp.exp2(last - acs_pair)).astype(BF16)
            state_ref[j] = jnp.exp2(last) * st + jnp.dot(bg_t, xw, preferred_element_type=F32)
            y_ref[:, ts] = y + dskip_ref[:, ts] * xs

    gw = SSM_WIDTH // SSM_GROUPS
    for g in range(SSM_GROUPS):
        gs = slice(g * gw, (g + 1) * gw)
        zz = z_ref[:, gs]
        yz = y_ref[:, gs] * (zz * jax.nn.sigmoid(zz))
        ms = jnp.mean(yz * yz, axis=-1, keepdims=True)
        o_ref[:, gs] = (yz * lax.rsqrt(ms + EPS) * normw_ref[:, gs]).astype(o_ref.dtype)


def ssd_mixer(proj, z_col, xbc_col, small, conv_w, conv_b, dt_bias, a_log, d_skip, norm_w, *, name):
    seq = proj.shape[0]
    L = SSD_CHUNK
    pad = lambda v: jnp.pad(v.astype(F32), (0, LANES - v.shape[0])).reshape(1, LANES)
    tril = jnp.asarray(np.tril(np.ones((L, L), np.float32)), BF16)
    hexp = jnp.asarray((np.arange(LANES)[:, None] == np.arange(SSM_HEADS * LANES)[None, :] // LANES)
                       .astype(np.float32), BF16)
    dskip_x = jnp.repeat(d_skip.astype(F32), SSM_HEAD_DIM).reshape(1, SSM_WIDTH)
    full = lambda shape: pl.BlockSpec(shape, lambda c: (0,) * len(shape))
    return pl.pallas_call(
        _ssd_body,
        grid=(seq // L,),
        in_specs=[pl.BlockSpec((L, SSM_WIDTH), lambda c, o=z_col // SSM_WIDTH: (c, o)),
                  pl.BlockSpec((L, CONV_DIM), lambda c, o=xbc_col // CONV_DIM: (c, o)),
                  pl.BlockSpec((L, LANES), lambda c: (c, 0)),
                  full((CONV_WIDTH, CONV_DIM)), full((1, CONV_DIM)), full((1, LANES)), full((1, LANES)),
                  full((1, SSM_WIDTH)), full((1, SSM_WIDTH)), full((L, L)), full((LANES, SSM_HEADS * LANES))],
        out_specs=pl.BlockSpec((L, SSM_WIDTH), lambda c: (c, 0)),
        out_shape=jax.ShapeDtypeStruct((seq, SSM_WIDTH), BF16),
        scratch_shapes=[pltpu.VMEM((L + SUBLANES, CONV_DIM), F32),
                        pltpu.VMEM((L, CONV_DIM), F32),
                        pltpu.VMEM((L, SSM_HEADS * LANES), F32),
                        pltpu.VMEM((L, SSM_HEADS * LANES), F32),
                        pltpu.VMEM((L, SSM_WIDTH), F32),
                        pltpu.VMEM((SSM_HEADS // 2, SSM_STATE, LANES), F32)],
        compiler_params=pltpu.CompilerParams(dimension_semantics=("arbitrary",),
                                             vmem_limit_bytes=VMEM_LIMIT_BYTES),
        name=name,
    )(proj, proj, small, conv_w, conv_b.reshape(1, CONV_DIM), pad(dt_bias), pad(a_log), dskip_x,
      norm_w.reshape(1, SSM_WIDTH).astype(F32), tril, hexp)


def _rope_tables(seq):
    inv = 1.0 / (ROPE_THETA ** (jnp.arange(0, HEAD_DIM, 2, dtype=F32) / HEAD_DIM))
    ang = jnp.arange(seq, dtype=F32)[:, None] * inv[None, :]
    ang = jnp.concatenate([ang, ang], axis=-1)
    return jnp.cos(ang), jnp.sin(ang)


def kernel(x, norm_mix, w_in, cmp_pos, cmp_w1, cmp_w2, attn_norm, conv_w, conv_b, dt_bias, a_log,
           d_skip, ssm_norm, w_out, norm_mlp, w_up, w_down, norm_final):
    batch, seq, _ = x.shape
    assert batch == 1
    cos, sin = _rope_tables(seq)
    half_sign = jnp.concatenate([-jnp.ones((HEAD_DIM // 2,), F32), jnp.ones((HEAD_DIM // 2,), F32)])
    sin_signed = sin * half_sign
    h = x.reshape(seq, D_MODEL)
    w_big = jnp.concatenate([w_in[:, :, :ATTN_WIDTH], w_in[:, :, W_Z_OFF:W_DT_OFF],
                             w_in[:, :, W_KV_OFF:W_GATE_OFF]], axis=2).astype(BF16)
    w_down16 = w_down.astype(BF16)
    w_small = jnp.concatenate(
        [w_in[:, :, W_DT_OFF:], w_in[:, :, W_GATE_OFF:W_Z_OFF],
         jnp.zeros((DEPTH, D_MODEL, SMALL_COLS - SSM_HEADS - GATE_COLS), F32)], axis=2).astype(BF16)
    for l in range(DEPTH):
        xn = rmsnorm(h, norm_mix[l], out_dtype=BF16, name=f"norm_mix{l}")
        proj = matmul(xn, w_big, l, name=f"in_proj{l}")
        small = matmul(xn, w_small, l, name=f"in_proj_small{l}")
        attn = nsa_mixer(proj, small[:, SSM_HEADS:SSM_HEADS + GATE_COLS], cmp_pos[l], cmp_w1[l], cmp_w2[l],
                         cos, sin_signed, tag=str(l))
        ssm = ssd_mixer(proj, Z_COL, XBC_COL, small, conv_w[l], conv_b[l], dt_bias[l], a_log[l], d_skip[l],
                        ssm_norm[l], name=f"ssd{l}")
        mix = norm_concat(attn, attn_norm[l], ssm, name=f"norm_concat{l}")
        h = matmul(mix, w_out, l, residual=h, bn=512, name=f"out_proj{l}")
        hn = rmsnorm(h, norm_mlp[l], out_dtype=BF16, name=f"norm_mlp{l}")
        u = matmul(hn, w_up, l, act="relu2", out_dtype=BF16, bn=512, name=f"mlp_up{l}")
        h = matmul(u, w_down16, l, residual=h, bk=2048, name=f"mlp_down{l}")
    out = rmsnorm(h, norm_final, out_dtype=F32, name="norm_final")
    return out.reshape(batch, seq, D_MODEL)
```

```python
import functools

import jax
import jax.numpy as jnp
import numpy as np
from jax import lax
from jax.experimental import pallas as pl
from jax.experimental.pallas import tpu as pltpu

F32 = jnp.float32
BF16 = jnp.bfloat16

D_MODEL = 4096
DEPTH = 2
ATTN_WIDTH = 2048
SSM_WIDTH = 2048
HEAD_DIM = 128
N_HEADS = 16
N_KV = 4
HEADS_PER_KV = 4
KV_WIDTH = 512
ROPE_THETA = 10000.0
CMP_LEN = 32
CMP_STRIDE = 16
SEL_BLOCK = 64
SEL_TOP = 16
WINDOW = 512
Q_BLOCK = 128
BIG = 1e9
SSM_HEAD_DIM = 64
SSM_HEADS = 32
SSM_GROUPS = 8
SSM_STATE = 128
CONV_WIDTH = 4
SSD_CHUNK = 256
CONV_DIM = SSM_WIDTH + 2 * SSM_GROUPS * SSM_STATE
D_FF = 4 * D_MODEL
EPS = 1e-5
NEG = -1e30

VMEM_LIMIT_BYTES = 52 * 1024 * 1024

GATE_COLS = 3 * N_HEADS
W_KV_OFF = ATTN_WIDTH
W_GATE_OFF = W_KV_OFF + 6 * KV_WIDTH
W_Z_OFF = W_GATE_OFF + GATE_COLS
W_XBC_OFF = W_Z_OFF + SSM_WIDTH
W_DT_OFF = W_XBC_OFF + CONV_DIM
Z_COL = ATTN_WIDTH
XBC_COL = Z_COL + SSM_WIDTH
KC_OFF = XBC_COL + CONV_DIM
VC_OFF, KS_OFF, VS_OFF, KW_OFF, VW_OFF = (KC_OFF + i * KV_WIDTH for i in range(1, 6))
SMALL_COLS = 128


def _matmul_body(*refs, nk, act, has_res):
    a_ref, b_ref = refs[0], refs[1]
    r_ref = refs[2] if has_res else None
    o_ref = refs[3] if has_res else refs[2]
    part = jnp.dot(a_ref[...], b_ref[...].astype(BF16), preferred_element_type=F32)

    def finish(acc):
        if act == "relu2":
            acc = jnp.square(jnp.maximum(acc, 0.0))
        if has_res:
            acc = acc + r_ref[...]
        o_ref[...] = acc.astype(o_ref.dtype)

    if nk == 1:
        finish(part)
    else:
        acc_ref = refs[-1]
        k = pl.program_id(2)

        @pl.when(k == 0)
        def _():
            acc_ref[...] = part

        @pl.when(k > 0)
        def _():
            acc_ref[...] += part

        @pl.when(k == nk - 1)
        def _():
            finish(acc_ref[...])


def matmul(a, b, layer, *, residual=None, act=None, out_dtype=F32, bm=1024, bn=1024, bk=None, name):
    m, kdim = a.shape
    _, _, n = b.shape
    bk = kdim if bk is None else bk
    bn = min(bn, n)
    assert m % bm == 0 and n % bn == 0 and kdim % bk == 0
    nk = kdim // bk
    in_specs = [pl.BlockSpec((bm, bk), lambda i, j, k: (i, k)),
                pl.BlockSpec((None, bk, bn), lambda i, j, k: (layer, k, j))]
    args = [a, b]
    if residual is not None:
        in_specs.append(pl.BlockSpec((bm, bn), lambda i, j, k: (i, j)))
        args.append(residual)
    return pl.pallas_call(
        functools.partial(_matmul_body, nk=nk, act=act, has_res=residual is not None),
        grid=(m // bm, n // bn, nk),
        in_specs=in_specs,
        out_specs=pl.BlockSpec((bm, bn), lambda i, j, k: (i, j)),
        out_shape=jax.ShapeDtypeStruct((m, n), out_dtype),
        scratch_shapes=[pltpu.VMEM((bm, bn), F32)] if nk > 1 else [],
        compiler_params=pltpu.CompilerParams(
            dimension_semantics=("parallel", "parallel", "arbitrary"),
            vmem_limit_bytes=VMEM_LIMIT_BYTES),
        name=name,
    )(*args)


def _rmsnorm_body(x_ref, w_ref, o_ref):
    x = x_ref[...]
    y = x * lax.rsqrt(jnp.mean(x * x, axis=-1, keepdims=True) + EPS)
    o_ref[...] = (y * w_ref[...]).astype(o_ref.dtype)


def rmsnorm(x, w, *, out_dtype, rows=256, name):
    m, d = x.shape
    return pl.pallas_call(
        _rmsnorm_body,
        grid=(m // rows,),
        in_specs=[pl.BlockSpec((rows, d), lambda i: (i, 0)),
                  pl.BlockSpec((1, d), lambda i: (0, 0))],
        out_specs=pl.BlockSpec((rows, d), lambda i: (i, 0)),
        out_shape=jax.ShapeDtypeStruct((m, d), out_dtype),
        compiler_params=pltpu.CompilerParams(dimension_semantics=("parallel",)),
        name=name,
    )(x, w.reshape(1, d))


def _norm_concat_body(a_ref, w_ref, s_ref, o_ref):
    x = a_ref[...]
    y = x * lax.rsqrt(jnp.mean(x * x, axis=-1, keepdims=True) + EPS)
    o_ref[:, :ATTN_WIDTH] = (y * w_ref[...]).astype(o_ref.dtype)
    o_ref[:, ATTN_WIDTH:] = s_ref[...]


def norm_concat(attn, w, ssm, *, rows=256, name):
    m = attn.shape[0]
    return pl.pallas_call(
        _norm_concat_body,
        grid=(m // rows,),
        in_specs=[pl.BlockSpec((rows, ATTN_WIDTH), lambda i: (i, 0)),
                  pl.BlockSpec((1, ATTN_WIDTH), lambda i: (0, 0)),
                  pl.BlockSpec((rows, SSM_WIDTH), lambda i: (i, 0))],
        out_specs=pl.BlockSpec((rows, D_MODEL), lambda i: (i, 0)),
        out_shape=jax.ShapeDtypeStruct((m, D_MODEL), BF16),
        compiler_params=pltpu.CompilerParams(dimension_semantics=("parallel",)),
        name=name,
    )(attn, w.reshape(1, ATTN_WIDTH), ssm)


SEL_CHUNK = 1024
SEL_SUB = 256
LOG2_E = 1.4426950408889634


def _split3(x):
    hi = x.astype(BF16)
    r1 = x - hi.astype(F32)
    mid = r1.astype(BF16)
    lo = (r1 - mid.astype(F32)).astype(BF16)
    return hi, mid, lo


def _dot_nt(a, b):
    return lax.dot_general(a, b, (((1,), (1,)), ((), ())), preferred_element_type=F32)


def _prep_body(q_ref, ks_ref, vs_ref, kw_ref, vw_ref, cos_ref, sin_ref, qo, kso, vso, kwo, vwo):
    step = pl.program_id(0)

    @pl.when(step == 0)
    def _():
        kwo[...] = jnp.zeros(kwo.shape, BF16)
        vwo[...] = jnp.zeros(vwo.shape, BF16)

    @pl.when(step > 0)
    def _():
        _prep_rows(q_ref, ks_ref, vs_ref, kw_ref, vw_ref, cos_ref, sin_ref, qo, kso, vso, kwo, vwo)


def _prep_rows(q_ref, ks_ref, vs_ref, kw_ref, vw_ref, cos_ref, sin_ref, qo, kso, vso, kwo, vwo):
    cos = cos_ref[...]
    sin = sin_ref[...]
    scale = HEAD_DIM ** -0.5 * LOG2_E

    def rope(t):
        return t * cos + pltpu.roll(t, HEAD_DIM // 2, 1) * sin

    for h in range(N_HEADS):
        sl = slice(h * HEAD_DIM, (h + 1) * HEAD_DIM)
        qo[:, sl] = (rope(q_ref[:, sl]) * scale).astype(BF16)
    ones = jnp.ones((q_ref.shape[0], HEAD_DIM), BF16)
    for g in range(N_KV):
        sl = slice(g * HEAD_DIM, (g + 1) * HEAD_DIM)
        kso[:, sl] = rope(ks_ref[:, sl]).astype(BF16)
        kwo[:, sl] = rope(kw_ref[:, sl]).astype(BF16)
        vso[:, 2 * g * HEAD_DIM:(2 * g + 1) * HEAD_DIM] = vs_ref[:, sl].astype(BF16)
        vso[:, (2 * g + 1) * HEAD_DIM:(2 * g + 2) * HEAD_DIM] = ones
        vwo[:, 2 * g * HEAD_DIM:(2 * g + 1) * HEAD_DIM] = vw_ref[:, sl].astype(BF16)
        vwo[:, (2 * g + 1) * HEAD_DIM:(2 * g + 2) * HEAD_DIM] = ones


def nsa_prep(proj, cos, sin_signed, *, name):
    seq = proj.shape[0]
    rows = WINDOW
    prev = lambda i: jnp.maximum(i - 1, 0)
    kvb = lambda off: pl.BlockSpec((rows, KV_WIDTH), lambda i, o=off // KV_WIDTH: (prev(i), o))
    return pl.pallas_call(
        _prep_body,
        grid=(seq // rows + 1,),
        in_specs=[pl.BlockSpec((rows, ATTN_WIDTH), lambda i: (prev(i), 0)),
                  kvb(KS_OFF), kvb(VS_OFF), kvb(KW_OFF), kvb(VW_OFF),
                  pl.BlockSpec((rows, HEAD_DIM), lambda i: (prev(i), 0)),
                  pl.BlockSpec((rows, HEAD_DIM), lambda i: (prev(i), 0))],
        out_specs=[pl.BlockSpec((rows, ATTN_WIDTH), lambda i: (prev(i), 0)),
                   pl.BlockSpec((rows, KV_WIDTH), lambda i: (prev(i), 0)),
                   pl.BlockSpec((rows, 2 * KV_WIDTH), lambda i: (prev(i), 0)),
                   pl.BlockSpec((rows, KV_WIDTH), lambda i: (i, 0)),
                   pl.BlockSpec((rows, 2 * KV_WIDTH), lambda i: (i, 0))],
        out_shape=[jax.ShapeDtypeStruct((seq, ATTN_WIDTH), BF16),
                   jax.ShapeDtypeStruct((seq, KV_WIDTH), BF16),
                   jax.ShapeDtypeStruct((seq, 2 * KV_WIDTH), BF16),
                   jax.ShapeDtypeStruct((seq + WINDOW, KV_WIDTH), BF16),
                   jax.ShapeDtypeStruct((seq + WINDOW, 2 * KV_WIDTH), BF16)],
        compiler_params=pltpu.CompilerParams(dimension_semantics=("arbitrary",),
                                             vmem_limit_bytes=VMEM_LIMIT_BYTES),
        name=name,
    )(proj, proj, proj, proj, proj, cos, sin_signed)


def _compress_body(t_ref, cos_ref, sin_ref, pos_ref, w1_ref, w2_ref, o_ref, tr_ref, *, rope, aug):
    seq = t_ref.shape[0]
    ncp = seq // CMP_STRIDE
    t = t_ref[...]
    if rope:
        t = t * cos_ref[...] + pltpu.roll(t, HEAD_DIM // 2, 1) * sin_ref[...]
    tr_ref[...] = t
    acc_a = jnp.zeros((ncp, HEAD_DIM), F32)
    acc_b = jnp.zeros((ncp, HEAD_DIM), F32)
    for p in range(CMP_STRIDE):
        tp = tr_ref[pl.ds(p, ncp, stride=CMP_STRIDE), :]
        wa = w1_ref[p * HEAD_DIM:(p + 1) * HEAD_DIM, :].astype(BF16)
        wb = w1_ref[(CMP_STRIDE + p) * HEAD_DIM:(CMP_STRIDE + p + 1) * HEAD_DIM, :].astype(BF16)
        acc_a += jnp.dot((tp + pos_ref[p:p + 1, :]).astype(BF16), wa, preferred_element_type=F32)
        acc_b += jnp.dot((tp + pos_ref[CMP_STRIDE + p:CMP_STRIDE + p + 1, :]).astype(BF16), wb,
                         preferred_element_type=F32)
    pre = acc_a + pltpu.roll(acc_b, ncp - 1, 0)
    hid = pre * jax.nn.sigmoid(pre)
    out = jnp.dot(hid.astype(BF16), w2_ref[...].astype(BF16), preferred_element_type=F32)
    o_ref[0, :, :HEAD_DIM] = out.astype(BF16)
    if aug:
        o_ref[0, :, HEAD_DIM:] = jnp.ones((ncp, HEAD_DIM), BF16)


def nsa_compress(proj, col_off, cos, sin_signed, pos, w1, w2, *, rope, aug, name):
    seq = proj.shape[0]
    ncp = seq // CMP_STRIDE
    width = 2 * HEAD_DIM if aug else HEAD_DIM
    return pl.pallas_call(
        functools.partial(_compress_body, rope=rope, aug=aug),
        grid=(N_KV,),
        in_specs=[pl.BlockSpec((seq, HEAD_DIM), lambda g, o=col_off // HEAD_DIM: (0, o + g)),
                  pl.BlockSpec((seq, HEAD_DIM), lambda g: (0, 0)),
                  pl.BlockSpec((seq, HEAD_DIM), lambda g: (0, 0)),
                  pl.BlockSpec((CMP_LEN, HEAD_DIM), lambda g: (0, 0)),
                  pl.BlockSpec((CMP_LEN * HEAD_DIM, HEAD_DIM), lambda g: (0, 0)),
                  pl.BlockSpec((HEAD_DIM, HEAD_DIM), lambda g: (0, 0))],
        out_specs=pl.BlockSpec((1, ncp, width), lambda g: (g, 0, 0)),
        out_shape=jax.ShapeDtypeStruct((N_KV, ncp, width), BF16),
        scratch_shapes=[pltpu.VMEM((seq, HEAD_DIM), F32)],
        compiler_params=pltpu.CompilerParams(dimension_semantics=("parallel",),
                                             vmem_limit_bytes=VMEM_LIMIT_BYTES),
        name=name,
    )(proj, cos, sin_signed, pos, w1, w2)


def _nsa_body(q_ref, gate_ref, kc_ref, vc_ref, ks_ref, vs_ref, kw_ref, vw_ref, ovt_ref, o_ref,
              sc_ref, m_ref, acc_ref, ow_ref, *, seq):
    i = pl.program_id(1)
    t0 = i * Q_BLOCK
    ncp = seq // CMP_STRIDE
    nsel = seq // SEL_BLOCK
    top = min(SEL_TOP, nsel)
    nwin = WINDOW + Q_BLOCK
    rows = [slice(r * Q_BLOCK, (r + 1) * Q_BLOCK) for r in range(HEADS_PER_KV)]

    q4 = jnp.concatenate([q_ref[:, r * HEAD_DIM:(r + 1) * HEAD_DIM] for r in range(HEADS_PER_KV)], axis=0)

    s_c = _dot_nt(q4, kc_ref[0])
    c_end = lax.broadcasted_iota(jnp.int32, (Q_BLOCK, ncp), 1) * CMP_STRIDE + (CMP_LEN - 1)
    tq_c = t0 + lax.broadcasted_iota(jnp.int32, (Q_BLOCK, ncp), 0)
    cmask = c_end <= tq_c
    vc = vc_ref[0]
    psum = jnp.zeros((Q_BLOCK, ncp), F32)
    o_cmp = []
    for r in range(HEADS_PER_KV):
        s = jnp.where(cmask, s_c[rows[r]], NEG)
        m = jnp.max(s, axis=-1, keepdims=True)
        p = jnp.where(cmask, jnp.exp2(s - m), 0.0)
        pn = p / jnp.maximum(jnp.sum(p, axis=-1, keepdims=True), 1e-30)
        psum = psum + pn
        o_cmp.append(jnp.dot(pn.astype(BF16), vc[:, :HEAD_DIM], preferred_element_type=F32))

    ovt = ovt_ref[...]
    imp = sum(_dot_nt(ovt, part) for part in _split3(psum))
    blk = lax.broadcasted_iota(jnp.int32, (nsel, Q_BLOCK), 0)
    cur = (t0 + lax.broadcasted_iota(jnp.int32, (nsel, Q_BLOCK), 1)) // SEL_BLOCK
    blkf = blk.astype(F32)
    forced = jnp.where(blk == 0, 1.0, jnp.where(blk == cur, 1.0, jnp.where(blk == cur - 1, 1.0, 0.0)))
    imp = jnp.where(forced > 0.5, -jnp.inf, jnp.where(blk > cur, -BIG, imp))
    sel_t = forced

    def topk_round(imp, sel_t):
        mx = jnp.max(imp, axis=0, keepdims=True)
        first = jnp.min(jnp.where(imp == mx, blkf, float(nsel)), axis=0, keepdims=True)
        hit = blkf == first
        return jnp.where(hit, -jnp.inf, imp), jnp.where(hit, 1.0, sel_t)

    w0 = pl.multiple_of(t0, Q_BLOCK)
    s_w = _dot_nt(q4, kw_ref[pl.ds(w0, nwin), :])
    vwin = vw_ref[pl.ds(w0, nwin), :]
    wpos = t0 - WINDOW + lax.broadcasted_iota(jnp.int32, (Q_BLOCK, nwin), 1)
    tq_w = t0 + lax.broadcasted_iota(jnp.int32, (Q_BLOCK, nwin), 0)
    wvalid = jnp.where(wpos <= tq_w, jnp.where(wpos > tq_w - WINDOW, jnp.where(wpos >= 0, 1.0, 0.0), 0.0), 0.0)
    wmask = wvalid > 0.5
    rounds = top - 3
    per_head = -(-rounds // HEADS_PER_KV)
    first_raw = []
    for r in range(HEADS_PER_KV):
        first_raw.append(_dot_nt(q4, ks_ref[pl.ds(r * SEL_SUB, SEL_SUB), :]))
        sm = jnp.where(wmask, s_w[rows[r]], NEG)
        pw = jnp.exp2(sm - jnp.max(sm, axis=-1, keepdims=True)).astype(BF16)
        res_w = jnp.dot(pw, vwin, preferred_element_type=F32)
        ow_ref[rows[r], :] = res_w[:, :HEAD_DIM] / res_w[:, HEAD_DIM:]
        for _ in range(min(per_head, rounds - r * per_head)):
            imp, sel_t = topk_round(imp, sel_t)
    sel_t = jnp.where(blk <= cur, sel_t, 0.0)
    sel = sel_t.T.astype(BF16)

    n_chunks = (t0 + Q_BLOCK + SEL_CHUNK - 1) // SEL_CHUNK
    diff = (lax.broadcasted_iota(jnp.int32, (nsel, SEL_SUB), 0)
            - lax.broadcasted_iota(jnp.int32, (nsel, SEL_SUB), 1) // SEL_BLOCK)
    lane_k = lax.broadcasted_iota(jnp.int32, (Q_BLOCK, SEL_SUB), 1)
    tq_s = t0 + lax.broadcasted_iota(jnp.int32, (Q_BLOCK, SEL_SUB), 0)
    lane_tiles = SEL_CHUNK // HEAD_DIM
    m_ref[...] = jnp.full(m_ref.shape, NEG, F32)

    def pass1(c, carry, raw=None):
        best = [m_ref[rows[r], :] for r in range(HEADS_PER_KV)]
        for u in range(SEL_CHUNK // SEL_SUB):
            k0 = pl.multiple_of(c * SEL_CHUNK + u * SEL_SUB, SEL_SUB)
            s = _dot_nt(q4, ks_ref[pl.ds(k0, SEL_SUB), :]) if raw is None else raw[u]
            expand = jnp.where(diff == k0 // SEL_BLOCK, 1.0, 0.0).astype(BF16)
            picked = jnp.dot(sel, expand, preferred_element_type=F32)
            picked = jnp.where(k0 + lane_k <= tq_s, picked, 0.0)
            mask = picked > 0.5
            for r in range(HEADS_PER_KV):
                sm = jnp.where(mask, s[rows[r]], NEG)
                sc_ref[c, rows[r], u * SEL_SUB:(u + 1) * SEL_SUB] = sm
                best[r] = jnp.maximum(best[r], jnp.maximum(sm[:, :HEAD_DIM], sm[:, HEAD_DIM:]))
        for r in range(HEADS_PER_KV):
            m_ref[rows[r], :] = best[r]
        return carry

    pass1(0, 0, first_raw)
    lax.fori_loop(1, n_chunks, pass1, 0)
    m_ref[...] = jnp.broadcast_to(jnp.max(m_ref[...], axis=-1, keepdims=True), m_ref.shape)

    acc_ref[...] = jnp.zeros(acc_ref.shape, F32)

    def pass2(c, carry):
        k0 = pl.multiple_of(c * SEL_CHUNK, SEL_CHUNK)
        p = jnp.exp2(sc_ref[c] - jnp.concatenate([m_ref[...]] * lane_tiles, axis=1)).astype(BF16)
        acc_ref[...] += jnp.dot(p, vs_ref[pl.ds(k0, SEL_CHUNK), :], preferred_element_type=F32)
        return carry

    lax.fori_loop(0, n_chunks, pass2, 0)

    gate = jax.nn.sigmoid(gate_ref[0])
    for r in range(HEADS_PER_KV):
        o_win = ow_ref[rows[r], :]
        res_s = acc_ref[rows[r], :]
        o_sel = res_s[:, :HEAD_DIM] / res_s[:, HEAD_DIM:]
        g_c = gate[:, r:r + 1]
        g_s = gate[:, HEADS_PER_KV + r:HEADS_PER_KV + r + 1]
        g_w = gate[:, 2 * HEADS_PER_KV + r:2 * HEADS_PER_KV + r + 1]
        o_ref[:, r * HEAD_DIM:(r + 1) * HEAD_DIM] = g_c * o_cmp[r] + g_s * o_sel + g_w * o_win


def _overlap_t(seq):
    ncp = seq // CMP_STRIDE
    nsel = seq // SEL_BLOCK
    c_start = np.arange(ncp) * CMP_STRIDE
    j_start = np.arange(nsel) * SEL_BLOCK
    ov = ((c_start[None, :] < j_start[:, None] + SEL_BLOCK) & (c_start[None, :] + CMP_LEN > j_start[:, None])
          & (np.arange(ncp)[None, :] < ncp - 1))
    return jnp.asarray(ov.astype(np.float32), BF16)


def nsa_attention(q_r, gates_g, kc, vc_aug, ks_r, vs_aug, kw_pad, vw_pad, *, name):
    seq = q_r.shape[0]
    nq = seq // Q_BLOCK
    ncp = seq // CMP_STRIDE
    grp = HEADS_PER_KV * HEAD_DIM
    return pl.pallas_call(
        functools.partial(_nsa_body, seq=seq),
        grid=(N_KV, nq),
        in_specs=[pl.BlockSpec((Q_BLOCK, grp), lambda g, i: (i, g)),
                  pl.BlockSpec((1, Q_BLOCK, HEAD_DIM), lambda g, i: (g, i, 0)),
                  pl.BlockSpec((1, ncp, HEAD_DIM), lambda g, i: (g, 0, 0)),
                  pl.BlockSpec((1, ncp, 2 * HEAD_DIM), lambda g, i: (g, 0, 0)),
                  pl.BlockSpec((seq, HEAD_DIM), lambda g, i: (0, g)),
                  pl.BlockSpec((seq, 2 * HEAD_DIM), lambda g, i: (0, g)),
                  pl.BlockSpec((seq + WINDOW, HEAD_DIM), lambda g, i: (0, g)),
                  pl.BlockSpec((seq + WINDOW, 2 * HEAD_DIM), lambda g, i: (0, g)),
                  pl.BlockSpec((seq // SEL_BLOCK, ncp), lambda g, i: (0, 0))],
        out_specs=pl.BlockSpec((Q_BLOCK, grp), lambda g, i: (i, g)),
        out_shape=jax.ShapeDtypeStruct((seq, ATTN_WIDTH), F32),
        scratch_shapes=[pltpu.VMEM((seq // SEL_CHUNK, HEADS_PER_KV * Q_BLOCK, SEL_CHUNK), F32),
                        pltpu.VMEM((HEADS_PER_KV * Q_BLOCK, HEAD_DIM), F32),
                        pltpu.VMEM((HEADS_PER_KV * Q_BLOCK, 2 * HEAD_DIM), F32),
                        pltpu.VMEM((HEADS_PER_KV * Q_BLOCK, HEAD_DIM), F32)],
        compiler_params=pltpu.CompilerParams(dimension_semantics=("parallel", "arbitrary"),
                                             vmem_limit_bytes=VMEM_LIMIT_BYTES),
        name=name,
    )(q_r, gates_g, kc, vc_aug, ks_r, vs_aug, kw_pad, vw_pad, _overlap_t(seq))


def nsa_mixer(proj, gate_logits, cmp_pos, cmp_w1, cmp_w2, cos, sin_signed, *, tag):
    seq = proj.shape[0]
    q_r, ks_r, vs_aug, kw_pad, vw_pad = nsa_prep(proj, cos, sin_signed, name="nsa_prep" + tag)
    kc = nsa_compress(proj, KC_OFF, cos, sin_signed, cmp_pos[0], cmp_w1[0], cmp_w2[0],
                      rope=True, aug=False, name="nsa_cmp_k" + tag)
    vc_aug = nsa_compress(proj, VC_OFF, cos, sin_signed, cmp_pos[1], cmp_w1[1], cmp_w2[1],
                          rope=False, aug=True, name="nsa_cmp_v" + tag)
    gl = gate_logits.reshape(seq, 3, N_KV, HEADS_PER_KV).transpose(2, 0, 1, 3).reshape(N_KV, seq, 3 * HEADS_PER_KV)
    gl = jnp.pad(gl, ((0, 0), (0, 0), (0, HEAD_DIM - 3 * HEADS_PER_KV)))
    return nsa_attention(q_r, gl, kc, vc_aug, ks_r, vs_aug, kw_pad, vw_pad, name="nsa_attention" + tag)


LANES = 128
SUBLANES = 8


def _exact_dot(x, m):
    return sum(jnp.dot(part, m, preferred_element_type=F32) for part in _split3(x))


def _exact_dot_left(m, x):
    return sum(jnp.dot(m, part, preferred_element_type=F32) for part in _split3(x))


def _ssd_body(z_ref, xbc_ref, dt_ref, convw_ref, convb_ref, dtb_ref, alog_ref, dskip_ref, normw_ref,
              tril_ref, hexp_ref, o_ref, xcat_ref, act_ref, acsx_ref, dtx_ref, y_ref, state_ref):
    c = pl.program_id(0)
    L = SSD_CHUNK
    half = lax.broadcasted_iota(jnp.int32, (L, LANES), 1) < SSM_HEAD_DIM

    @pl.when(c == 0)
    def _():
        xcat_ref[0:SUBLANES, :] = jnp.zeros((SUBLANES, CONV_DIM), F32)
        state_ref[...] = jnp.zeros(state_ref.shape, F32)

    xcat_ref[SUBLANES:SUBLANES + L, :] = xbc_ref[...]
    slab = 512
    for j in range(CONV_DIM // slab):
        cs = slice(j * slab, (j + 1) * slab)
        conv = convb_ref[:, cs] + xcat_ref[pl.ds(SUBLANES - 3, L), cs] * convw_ref[0:1, cs]
        for k in range(1, CONV_WIDTH):
            conv = conv + xcat_ref[pl.ds(SUBLANES - 3 + k, L), cs] * convw_ref[k:k + 1, cs]
        act_ref[:, cs] = conv * jax.nn.sigmoid(conv)
    xcat_ref[0:SUBLANES, :] = xcat_ref[L:L + SUBLANES, :]

    raw = dt_ref[...] + dtb_ref[...]
    dtv = jnp.maximum(raw, 0.0) + jnp.log1p(jnp.exp(-jnp.abs(raw)))
    a = dtv * (-jnp.exp(alog_ref[...]) * LOG2_E)
    acs = _exact_dot_left(tril_ref[...], a)
    acs_t = acs.T
    hexp = hexp_ref[...]
    acsx_ref[...] = _exact_dot(acs, hexp)
    dtx_ref[...] = _exact_dot(dtv, hexp)

    tri = lax.broadcasted_iota(jnp.int32, (L, L), 0) >= lax.broadcasted_iota(jnp.int32, (L, L), 1)
    bcol = SSM_WIDTH
    ccol = SSM_WIDTH + SSM_GROUPS * SSM_STATE
    for g in range(SSM_GROUPS):
        bg = act_ref[:, bcol + g * SSM_STATE:bcol + (g + 1) * SSM_STATE]
        cg = act_ref[:, ccol + g * SSM_STATE:ccol + (g + 1) * SSM_STATE].astype(BF16)
        cb = _dot_nt(cg, bg.astype(BF16))
        bg_t = bg.T.astype(BF16)
        for jj in range(2):
            j = 2 * g + jj
            h0, h1 = 2 * j, 2 * j + 1
            ts = slice(j * LANES, (j + 1) * LANES)
            xs = act_ref[:, ts]
            col0 = acsx_ref[:, h0 * LANES:(h0 + 1) * LANES]
            col1 = acsx_ref[:, h1 * LANES:(h1 + 1) * LANES]
            acs_pair = jnp.where(half, col0, col1)
            dt_pair = jnp.where(half, dtx_ref[:, h0 * LANES:(h0 + 1) * LANES], dtx_ref[:, h1 * LANES:(h1 + 1) * LANES])
            xdt = xs * dt_pair
            xdt16 = xdt.astype(BF16)
            zero16 = jnp.zeros_like(xdt16)
            seg0 = jnp.concatenate([col0, col0], axis=1) - acs_t[h0:h0 + 1, :]
            seg1 = jnp.concatenate([col1, col1], axis=1) - acs_t[h1:h1 + 1, :]
            g0 = (cb * jnp.exp2(jnp.where(tri, seg0, NEG))).astype(BF16)
            g1 = (cb * jnp.exp2(jnp.where(tri, seg1, NEG))).astype(BF16)
            y = (jnp.dot(g0, jnp.where(half, xdt16, zero16), preferred_element_type=F32)
                 + jnp.dot(g1, jnp.where(half, zero16, xdt16), preferred_element_type=F32))
            st = state_ref[j]
            y = y + jnp.dot(cg, st.astype(BF16), preferred_element_type=F32) * jnp.exp2(acs_pair)
            last = acs_pair[L - 1:L, :]
            xw = (xdt * jnp.exp2(last - acs_pair)).astype(BF16)
            state_ref[j] = jnp.exp2(last) * st + jnp.dot(bg_t, xw, preferred_element_type=F32)
            y_ref[:, ts] = y + dskip_ref[:, ts] * xs

    gw = SSM_WIDTH // SSM_GROUPS
    for g in range(SSM_GROUPS):
        gs = slice(g * gw, (g + 1) * gw)
        zz = z_ref[:, gs]
        yz = y_ref[:, gs] * (zz * jax.nn.sigmoid(zz))
        ms = jnp.mean(yz * yz, axis=-1, keepdims=True)
        o_ref[:, gs] = (yz * lax.rsqrt(ms + EPS) * normw_ref[:, gs]).astype(o_ref.dtype)


def ssd_mixer(proj, z_col, xbc_col, small, conv_w, conv_b, dt_bias, a_log, d_skip, norm_w, *, name):
    seq = proj.shape[0]
    L = SSD_CHUNK
    pad = lambda v: jnp.pad(v.astype(F32), (0, LANES - v.shape[0])).reshape(1, LANES)
    tril = jnp.asarray(np.tril(np.ones((L, L), np.float32)), BF16)
    hexp = jnp.asarray((np.arange(LANES)[:, None] == np.arange(SSM_HEADS * LANES)[None, :] // LANES)
                       .astype(np.float32), BF16)
    dskip_x = jnp.repeat(d_skip.astype(F32), SSM_HEAD_DIM).reshape(1, SSM_WIDTH)
    full = lambda shape: pl.BlockSpec(shape, lambda c: (0,) * len(shape))
    return pl.pallas_call(
        _ssd_body,
        grid=(seq // L,),
        in_specs=[pl.BlockSpec((L, SSM_WIDTH), lambda c, o=z_col // SSM_WIDTH: (c, o)),
                  pl.BlockSpec((L, CONV_DIM), lambda c, o=xbc_col // CONV_DIM: (c, o)),
                  pl.BlockSpec((L, LANES), lambda c: (c, 0)),
                  full((CONV_WIDTH, CONV_DIM)), full((1, CONV_DIM)), full((1, LANES)), full((1, LANES)),
                  full((1, SSM_WIDTH)), full((1, SSM_WIDTH)), full((L, L)), full((LANES, SSM_HEADS * LANES))],
        out_specs=pl.BlockSpec((L, SSM_WIDTH), lambda c: (c, 0)),
        out_shape=jax.ShapeDtypeStruct((seq, SSM_WIDTH), BF16),
        scratch_shapes=[pltpu.VMEM((L + SUBLANES, CONV_DIM), F32),
                        pltpu.VMEM((L, CONV_DIM), F32),
                        pltpu.VMEM((L, SSM_HEADS * LANES), F32),
                        pltpu.VMEM((L, SSM_HEADS * LANES), F32),
                        pltpu.VMEM((L, SSM_WIDTH), F32),
                        pltpu.VMEM((SSM_HEADS // 2, SSM_STATE, LANES), F32)],
        compiler_params=pltpu.CompilerParams(dimension_semantics=("arbitrary",),
                                             vmem_limit_bytes=VMEM_LIMIT_BYTES),
        name=name,
    )(proj, proj, small, conv_w, conv_b.reshape(1, CONV_DIM), pad(dt_bias), pad(a_log), dskip_x,
      norm_w.reshape(1, SSM_WIDTH).astype(F32), tril, hexp)


def _rope_tables(seq):
    inv = 1.0 / (ROPE_THETA ** (jnp.arange(0, HEAD_DIM, 2, dtype=F32) / HEAD_DIM))
    ang = jnp.arange(seq, dtype=F32)[:, None] * inv[None, :]
    ang = jnp.concatenate([ang, ang], axis=-1)
    return jnp.cos(ang), jnp.sin(ang)


def kernel(x, norm_mix, w_in, cmp_pos, cmp_w1, cmp_w2, attn_norm, conv_w, conv_b, dt_bias, a_log,
           d_skip, ssm_norm, w_out, norm_mlp, w_up, w_down, norm_final):
    batch, seq, _ = x.shape
    assert batch == 1
    cos, sin = _rope_tables(seq)
    half_sign = jnp.concatenate([-jnp.ones((HEAD_DIM // 2,), F32), jnp.ones((HEAD_DIM // 2,), F32)])
    sin_signed = sin * half_sign
    h = x.reshape(seq, D_MODEL)
    w_big = jnp.concatenate([w_in[:, :, :ATTN_WIDTH], w_in[:, :, W_Z_OFF:W_DT_OFF],
                             w_in[:, :, W_KV_OFF:W_GATE_OFF]], axis=2).astype(BF16)
    w_down16 = w_down.astype(BF16)
    w_small = jnp.concatenate(
        [w_in[:, :, W_DT_OFF:], w_in[:, :, W_GATE_OFF:W_Z_OFF],
         jnp.zeros((DEPTH, D_MODEL, SMALL_COLS - SSM_HEADS - GATE_COLS), F32)], axis=2).astype(BF16)
    for l in range(DEPTH):
        xn = rmsnorm(h, norm_mix[l], out_dtype=BF16, name=f"norm_mix{l}")
        proj = matmul(xn, w_big, l, name=f"in_proj{l}")
        small = matmul(xn, w_small, l, name=f"in_proj_small{l}")
        attn = nsa_mixer(proj, small[:, SSM_HEADS:SSM_HEADS + GATE_COLS], cmp_pos[l], cmp_w1[l], cmp_w2[l],
                         cos, sin_signed, tag=str(l))
        ssm = ssd_mixer(proj, Z_COL, XBC_COL, small, conv_w[l], conv_b[l], dt_bias[l], a_log[l], d_skip[l],
                        ssm_norm[l], name=f"ssd{l}")
        mix = norm_concat(attn, attn_norm[l], ssm, name=f"norm_concat{l}")
        h = matmul(mix, w_out, l, residual=h, bn=512, name=f"out_proj{l}")
        hn = rmsnorm(h, norm_mlp[l], out_dtype=BF16, name=f"norm_mlp{l}")
        u = matmul(hn, w_up, l, act="relu2", out_dtype=BF16, bn=512, name=f"mlp_up{l}")
        h = matmul(u, w_down16, l, residual=h, bk=2048, name=f"mlp_down{l}")
    out = rmsnorm(h, norm_final, out_dtype=F32, name="norm_final")
    return out.reshape(batch, seq, D_MODEL)
```
